```python
import math
import jax, jax.numpy as jnp
from jax import lax
import numpy as np

D_MODEL = 1024
BATCH = 8
SEQ = 2048
DEPTH = 2

EPS = 1e-6
CONV_CH = D_MODEL
CONV_K = 31
CONV_PAD = (CONV_K - 1) // 2
ATTN_HEADS = 8
ATTN_DH = 64
ATTN_QK = ATTN_HEADS * 2 * ATTN_DH
ATTN_V = ATTN_HEADS * 2 * ATTN_DH
Q_BLOCK = 128
SGU_WIDTH = D_MODEL
SGU_GROUPS = 8
SGU_GROUP_DIM = SGU_WIDTH // SGU_GROUPS
CHUNK = 128
N_BRANCH = 3
D_FF = 4 * D_MODEL
COL_SIZES = (CONV_CH, CONV_CH, ATTN_QK, ATTN_QK, ATTN_V, SGU_WIDTH, SGU_WIDTH, N_BRANCH * D_MODEL)
SPLITS = tuple(int(s) for s in np.cumsum(COL_SIZES)[:-1])
W_IN_COLS = int(sum(COL_SIZES))

kernel_name = "hybrid_conv_diffattn_sgu_encoder"


def rms_norm(x, g):
    xf = x.astype(jnp.float32)
    y = xf * lax.rsqrt(jnp.mean(xf * xf, axis=-1, keepdims=True) + EPS)
    return (y * g.astype(jnp.float32)).astype(x.dtype)


def layer_norm(x, g, b):
    xf = x.astype(jnp.float32)
    mu = jnp.mean(xf, axis=-1, keepdims=True)
    xc = xf - mu
    y = xc * lax.rsqrt(jnp.mean(xc * xc, axis=-1, keepdims=True) + EPS)
    return (y * g.astype(jnp.float32) + b.astype(jnp.float32)).astype(x.dtype)


def alibi_slopes(n_heads):
    return jnp.exp2(-8.0 * jnp.arange(1, n_heads + 1, dtype=jnp.float32) / n_heads)


def depthwise_conv(x, w, b):
    c = x.shape[-1]
    y = lax.conv_general_dilated(
        x, w[:, None, :].astype(x.dtype), window_strides=(1,),
        padding=[(CONV_PAD, CONV_PAD)],
        dimension_numbers=("NWC", "WIO", "NWC"), feature_group_count=c)
    return y + b.astype(x.dtype)


def diff_attention(q, k, v, lam, slopes):
    b, s = q.shape[0], q.shape[1]
    nb = s // Q_BLOCK
    scale = ATTN_DH ** -0.5
    pos = jnp.arange(s, dtype=jnp.float32)
    qb = q.reshape(b, nb, Q_BLOCK, ATTN_HEADS, 2, ATTN_DH).transpose(1, 0, 2, 3, 4, 5)
    qpos = pos.reshape(nb, Q_BLOCK)
    kf = k.astype(jnp.float32)
    vf = v.astype(jnp.float32)

    def block(args):
        qblk, tq = args
        sc = jnp.einsum("bqhjd,bkhjd->bhjqk", qblk.astype(jnp.float32), kf) * scale
        dist = jnp.abs(tq[:, None] - pos[None, :])
        sc = sc - slopes[None, :, None, None, None] * dist[None, None, None]
        p = jax.nn.softmax(sc, axis=-1)
        a = p[:, :, 0] - lam * p[:, :, 1]
        return jnp.einsum("bhqk,bkhe->bqhe", a, vf)

    o = lax.map(block, (qb, qpos))
    return o.transpose(1, 0, 2, 3, 4).reshape(b, s, ATTN_HEADS, 2 * ATTN_DH).astype(q.dtype)


def spatial_gating(u, v, w_s, b_s):
    b, s, _ = v.shape
    n = s // CHUNK
    vc = v.reshape(b, n, CHUNK, SGU_GROUPS, SGU_GROUP_DIM)
    mixed = jnp.einsum("gts,bnsgc->bntgc", w_s.astype(v.dtype), vc) + b_s.T.astype(v.dtype)[None, None, :, :, None]
    return u * mixed.reshape(b, s, SGU_WIDTH)


def setup_inputs(seed: int = 0) -> dict:
    key = jax.random.key(seed)
    ks = jax.random.split(key, 32)
    f32 = jnp.float32
    L, D = DEPTH, D_MODEL

    def nrm(k, shape, scale):
        return jax.random.normal(k, shape, f32) * scale

    def gain(k, shape):
        return 1.0 + 0.02 * jax.random.normal(k, shape, f32)

    return {
        "x": jax.random.normal(ks[0], (BATCH, SEQ, D), f32),
        "norm_mix_pre": gain(ks[1], (L, D)),
        "norm_mix_post": gain(ks[2], (L, D)),
        "w_in": nrm(ks[3], (L, D, W_IN_COLS), D ** -0.5),
        "b_gate": nrm(ks[4], (L, N_BRANCH * D), 0.02),
        "conv_w": nrm(ks[5], (L, CONV_K, CONV_CH), CONV_K ** -0.5),
        "conv_b": nrm(ks[6], (L, CONV_CH), 0.02),
        "conv_ln_g": gain(ks[7], (L, CONV_CH)),
        "conv_ln_b": nrm(ks[8], (L, CONV_CH), 0.02),
        "lam_q1": nrm(ks[9], (L, ATTN_DH), 0.1),
        "lam_k1": nrm(ks[10], (L, ATTN_DH), 0.1),
        "lam_q2": nrm(ks[11], (L, ATTN_DH), 0.1),
        "lam_k2": nrm(ks[12], (L, ATTN_DH), 0.1),
        "subln_g": gain(ks[13], (L, 2 * ATTN_DH)),
        "sgu_ln_g": gain(ks[14], (L, SGU_WIDTH)),
        "sgu_ln_b": nrm(ks[15], (L, SGU_WIDTH), 0.02),
        "sgu_w": nrm(ks[16], (L, SGU_GROUPS, CHUNK, CHUNK), CHUNK ** -0.5),
        "sgu_b": gain(ks[17], (L, SGU_GROUPS, CHUNK)),
        "w_proj_conv": nrm(ks[18], (L, CONV_CH, D), CONV_CH ** -0.5),
        "w_proj_attn": nrm(ks[19], (L, ATTN_V, D), ATTN_V ** -0.5),
        "w_proj_sgu": nrm(ks[20], (L, SGU_WIDTH, D), SGU_WIDTH ** -0.5),
        "w_out": nrm(ks[21], (L, D, D), D ** -0.5),
        "norm_ffn_pre": gain(ks[22], (L, D)),
        "norm_ffn_post": gain(ks[23], (L, D)),
        "w_ffn_up": nrm(ks[24], (L, D, D_FF), D ** -0.5),
        "w_ffn_down": nrm(ks[25], (L, D_FF, D), D_FF ** -0.5),
    }


def reference(x, norm_mix_pre, norm_mix_post, w_in, b_gate, conv_w, conv_b, conv_ln_g, conv_ln_b,
              lam_q1, lam_k1, lam_q2, lam_k2, subln_g, sgu_ln_g, sgu_ln_b, sgu_w, sgu_b,
              w_proj_conv, w_proj_attn, w_proj_sgu, w_out, norm_ffn_pre, norm_ffn_post,
              w_ffn_up, w_ffn_down):
    b, s, _ = x.shape
    slopes = alibi_slopes(ATTN_HEADS)
    for l in range(DEPTH):
        h = rms_norm(x, norm_mix_pre[l])
        z = h @ w_in[l]
        za, zb, q, k, v, su, sv, gl = jnp.split(z, SPLITS, axis=-1)

        a = za * jax.nn.sigmoid(zb)
        a = depthwise_conv(a, conv_w[l], conv_b[l])
        a = jax.nn.silu(layer_norm(a, conv_ln_g[l], conv_ln_b[l]))
        y_a = a @ w_proj_conv[l]

        lam_init = 0.8 - 0.6 * math.exp(-0.3 * l)
        lam = (jnp.exp(jnp.sum(lam_q1[l].astype(jnp.float32) * lam_k1[l].astype(jnp.float32)))
               - jnp.exp(jnp.sum(lam_q2[l].astype(jnp.float32) * lam_k2[l].astype(jnp.float32)))
               + lam_init)
        o = diff_attention(q.reshape(b, s, ATTN_HEADS, 2, ATTN_DH),
                           k.reshape(b, s, ATTN_HEADS, 2, ATTN_DH),
                           v.reshape(b, s, ATTN_HEADS, 2 * ATTN_DH), lam, slopes)
        o = rms_norm(o, subln_g[l]) * (1.0 - lam_init)
        y_b = o.reshape(b, s, ATTN_V) @ w_proj_attn[l]

        gu = jax.nn.gelu(su, approximate=False)
        gv = layer_norm(jax.nn.gelu(sv, approximate=False), sgu_ln_g[l], sgu_ln_b[l])
        y_c = spatial_gating(gu, gv, sgu_w[l], sgu_b[l]) @ w_proj_sgu[l]

        g_a, g_b, g_c = jnp.split(jax.nn.sigmoid(gl + b_gate[l]), N_BRANCH, axis=-1)
        mix = (g_a * y_a + g_b * y_b + g_c * y_c) @ w_out[l]
        x = x + rms_norm(mix, norm_mix_post[l])

        h = rms_norm(x, norm_ffn_pre[l])
        f = jnp.square(jax.nn.relu(h @ w_ffn_up[l])) @ w_ffn_down[l]
        x = x + rms_norm(f, norm_ffn_post[l])
    return x
```

```python
import functools
import math

import jax
import jax.numpy as jnp
from jax import lax
from jax.experimental import pallas as pl
from jax.experimental.pallas import tpu as pltpu

F32 = jnp.float32
BF16 = jnp.bfloat16

D_MODEL = 1024
BATCH = 8
SEQ = 2048
DEPTH = 2
TOKENS = BATCH * SEQ
EPS = 1e-6

CONV_K = 31
CONV_PAD = (CONV_K - 1) // 2
HEADS = 8
DH = 64
HEAD_W = 2 * DH
SGU_GROUPS = 8
CHUNK = 128
D_FF = 4 * D_MODEL

Z_A, Z_B, Z_Q, Z_K, Z_V, Z_SU, Z_SV, Z_GA, Z_GB, Z_GC = (i * D_MODEL for i in range(10))
Z_COLS = 10 * D_MODEL

LANES = 128
HALO = 16
LOG2E = 1.4426950408889634

VMEM_LIMIT = 56 * 1024 * 1024


def _params(semantics):
    return pltpu.CompilerParams(dimension_semantics=semantics, vmem_limit_bytes=VMEM_LIMIT)


def _resident(shape):
    zeros = (0,) * len(shape)
    return pl.BlockSpec(shape, lambda *_: zeros, pipeline_mode=pl.Buffered(1))


def _rms(x, g):
    ms = jnp.mean(x * x, axis=-1, keepdims=True)
    return x * lax.rsqrt(ms + EPS) * g


def _gelu(x):
    return 0.5 * x * (1.0 + lax.erf(x * (2.0 ** -0.5)))


PI_TM = 2048
PI_TN = 1024
PI_RC = 256


def _proj_in_kernel(x_ref, g_ref, cs_ref, w_ref, z_ref, h_ref):
    @pl.when(pl.program_id(1) == 0)
    def _():
        for r in range(PI_TM // PI_RC):
            rows = slice(r * PI_RC, (r + 1) * PI_RC)
            h_ref[rows, :] = _rms(x_ref[rows, :], g_ref[...]).astype(BF16)

    for r in range(PI_TM // PI_RC):
        rows = slice(r * PI_RC, (r + 1) * PI_RC)
        acc = jnp.dot(h_ref[rows, :], w_ref[...], preferred_element_type=F32)
        z_ref[rows, :] = (acc * cs_ref[...]).astype(BF16)


def _proj_in(x2, g, colscale, w):
    return pl.pallas_call(
        _proj_in_kernel,
        grid=(TOKENS // PI_TM, Z_COLS // PI_TN),
        in_specs=[
            pl.BlockSpec((PI_TM, D_MODEL), lambda i, j: (i, 0)),
            _resident((1, D_MODEL)),
            pl.BlockSpec((1, PI_TN), lambda i, j: (0, j)),
            pl.BlockSpec((D_MODEL, PI_TN), lambda i, j: (0, j)),
        ],
        out_specs=pl.BlockSpec((PI_TM, PI_TN), lambda i, j: (i, j)),
        out_shape=jax.ShapeDtypeStruct((TOKENS, Z_COLS), BF16),
        scratch_shapes=[pltpu.VMEM((PI_TM, D_MODEL), BF16)],
        compiler_params=_params(("parallel", "arbitrary")),
        name="proj_in",
    )(x2, g, colscale, w)


CV_TS = 512
CV_R = 128
CV_NC = D_MODEL // LANES


def _conv_kernel(za_ref, zb_ref, zap_ref, zbp_ref, zan_ref, zbn_ref, gl_ref, bg_ref,
                 cw_ref, cb_ref, lng_ref, lnb_ref, wp_ref, out_ref, a_scr, c_scr, act_scr):
    s = pl.program_id(1)
    has_prev = s > 0
    has_next = s < pl.num_programs(1) - 1

    def glu(a, b):
        return a.astype(F32) * jax.nn.sigmoid(b.astype(F32))

    for c in range(CV_NC):
        cs = slice(c * LANES, (c + 1) * LANES)
        prev = glu(zap_ref[0, :, cs], zbp_ref[0, :, cs])
        nxt = glu(zan_ref[0, :, cs], zbn_ref[0, :, cs])
        a_scr[c, 0:HALO, :] = jnp.where(has_prev, prev, 0.0)
        a_scr[c, HALO:HALO + CV_TS, :] = glu(za_ref[0, :, cs], zb_ref[0, :, cs])
        a_scr[c, HALO + CV_TS:, :] = jnp.where(has_next, nxt, 0.0)

    def chunk_body(c, carry):
        for r in range(CV_TS // CV_R):
            acc = jnp.zeros((CV_R, LANES), F32)
            for t in range(CONV_K):
                start = r * CV_R + HALO - CONV_PAD + t
                acc = acc + cw_ref[c, pl.ds(t, 1), :] * a_scr[c, pl.ds(start, CV_R), :]
            c_scr[c, r * CV_R:(r + 1) * CV_R, :] = acc + cb_ref[pl.ds(c, 1), :]
        return carry

    lax.fori_loop(0, CV_NC, chunk_body, 0)

    tot = c_scr[0]
    for c in range(1, CV_NC):
        tot = tot + c_scr[c]
    mu = jnp.sum(tot, axis=-1, keepdims=True) * (1.0 / D_MODEL)
    sq = jnp.zeros((CV_TS, LANES), F32)
    for c in range(CV_NC):
        d = c_scr[c] - mu
        sq = sq + d * d
    rstd = lax.rsqrt(jnp.sum(sq, axis=-1, keepdims=True) * (1.0 / D_MODEL) + EPS)
    for c in range(CV_NC):
        cs = slice(c * LANES, (c + 1) * LANES)
        y = (c_scr[c] - mu) * rstd * lng_ref[:, cs] + lnb_ref[:, cs]
        act_scr[:, cs] = (y * jax.nn.sigmoid(y)).astype(BF16)

    y = jnp.dot(act_scr[...], wp_ref[...], preferred_element_type=F32)
    gate = jax.nn.sigmoid(gl_ref[0].astype(F32) + bg_ref[...])
    out_ref[0] = (gate * y).astype(BF16)


def _conv_branch(z3, bg, cw, cb, lng, lnb, wp):
    nblk = CV_TS // HALO
    last = SEQ // HALO - 1

    def cur(col):
        return pl.BlockSpec((1, CV_TS, D_MODEL), lambda b, s: (b, s, col))

    def prev(col):
        return pl.BlockSpec((1, HALO, D_MODEL), lambda b, s: (b, jnp.maximum(s * nblk - 1, 0), col))

    def nxt(col):
        return pl.BlockSpec((1, HALO, D_MODEL), lambda b, s: (b, jnp.minimum((s + 1) * nblk, last), col))

    ca, cbk, cg = Z_A // D_MODEL, Z_B // D_MODEL, Z_GA // D_MODEL
    return pl.pallas_call(
        _conv_kernel,
        grid=(BATCH, SEQ // CV_TS),
        in_specs=[
            cur(ca), cur(cbk), prev(ca), prev(cbk), nxt(ca), nxt(cbk), cur(cg),
            _resident((1, D_MODEL)),
            _resident((CV_NC, CONV_K, LANES)),
            _resident((CV_NC, LANES)),
            _resident((1, D_MODEL)),
            _resident((1, D_MODEL)),
            _resident((D_MODEL, D_MODEL)),
        ],
        out_specs=pl.BlockSpec((1, CV_TS, D_MODEL), lambda b, s: (b, s, 0)),
        out_shape=jax.ShapeDtypeStruct((BATCH, SEQ, D_MODEL), BF16),
        scratch_shapes=[
            pltpu.VMEM((CV_NC, CV_TS + 2 * HALO, LANES), F32),
            pltpu.VMEM((CV_NC, CV_TS, LANES), F32),
            pltpu.VMEM((CV_TS, D_MODEL), BF16),
        ],
        compiler_params=_params(("parallel", "arbitrary")),
        name="conv_branch",
    )(z3, z3, z3, z3, z3, z3, z3, bg, cw, cb, lng, lnb, wp)


AT_TQ = 256
AT_STRIP = 2 * SEQ - AT_TQ


def _attn_kernel(lam_init, lam_ref, slope_ref, q_ref, k_ref, v_ref, g_ref, o_ref,
                 km1_scr, km2_scr, vt_scr, bias_scr):
    qi = pl.program_id(2)

    @pl.when(qi == 0)
    def _():
        k = k_ref[0]
        lane = lax.broadcasted_iota(jnp.int32, k.shape, 1)
        zero = jnp.zeros_like(k)
        km1_scr[...] = jnp.where(lane < DH, k, zero)
        km2_scr[...] = jnp.where(lane >= DH, k, zero)
        vt_scr[...] = v_ref[0].astype(F32).T.astype(BF16)
        row = lax.broadcasted_iota(jnp.int32, (AT_STRIP, AT_TQ), 0)
        col = lax.broadcasted_iota(jnp.int32, (AT_STRIP, AT_TQ), 1)
        dist = jnp.abs(col + (SEQ - AT_TQ) - row).astype(F32)
        bias_scr[...] = -(slope_ref[0] * dist)

    lam_v = lam_ref[...]
    dots = jnp.sum(lam_v[0:2, :] * lam_v[2:4, :], axis=-1, keepdims=True)
    e = jnp.exp(dots)
    lam = e[0:1, :] - e[1:2, :] + lam_init

    q = q_ref[0]
    start = pl.multiple_of((SEQ - AT_TQ) - qi * AT_TQ, AT_TQ)
    bias = bias_scr[pl.ds(start, SEQ), :]

    def one_map(km_scr):
        s = lax.dot_general(km_scr[...], q, (((1,), (1,)), ((), ())),
                            preferred_element_type=F32) + bias
        m = jnp.max(s, axis=0, keepdims=True)
        p = jnp.exp2(s - m)
        denom = jnp.sum(p, axis=0, keepdims=True)
        ot = jnp.dot(vt_scr[...], p.astype(BF16), preferred_element_type=F32)
        return ot / denom

    ot = one_map(km1_scr) - lam * one_map(km2_scr)
    ms = jnp.mean(ot * ot, axis=0, keepdims=True)
    ot = ot * lax.rsqrt(ms + EPS)
    o_ref[0] = (ot.T * g_ref[...]).astype(BF16)


def _attention(z3, lam_vecs, slopes, gain, lam_init):
    qc, kc, vc = Z_Q // HEAD_W, Z_K // HEAD_W, Z_V // HEAD_W
    return pl.pallas_call(
        functools.partial(_attn_kernel, lam_init),
        grid=(BATCH, HEADS, SEQ // AT_TQ),
        in_specs=[
            _resident((4, DH)),
            pl.BlockSpec((1, 1, AT_TQ), lambda b, h, i: (h, 0, 0)),
            pl.BlockSpec((1, AT_TQ, HEAD_W), lambda b, h, i: (b, i, qc + h)),
            pl.BlockSpec((1, SEQ, HEAD_W), lambda b, h, i: (b, 0, kc + h)),
            pl.BlockSpec((1, SEQ, HEAD_W), lambda b, h, i: (b, 0, vc + h)),
            _resident((1, HEAD_W)),
        ],
        out_specs=pl.BlockSpec((1, AT_TQ, HEAD_W), lambda b, h, i: (b, i, h)),
        out_shape=jax.ShapeDtypeStruct((BATCH, SEQ, HEADS * HEAD_W), BF16),
        scratch_shapes=[
            pltpu.VMEM((SEQ, HEAD_W), BF16),
            pltpu.VMEM((SEQ, HEAD_W), BF16),
            pltpu.VMEM((HEAD_W, SEQ), BF16),
            pltpu.VMEM((AT_STRIP, AT_TQ), F32),
        ],
        compiler_params=_params(("parallel", "arbitrary", "arbitrary")),
        name="diff_attention",
    )(lam_vecs, slopes, z3, z3, z3, gain)


SG_TS = 256


def _sgu_kernel(su_ref, sv_ref, gl_ref, bg_ref, lng_ref, lnb_ref, ws_ref, bs_ref, wp_ref,
                out_ref, u_scr, gv_scr, sg_scr):
    v = _gelu(sv_ref[...].astype(F32))
    mu = jnp.mean(v, axis=-1, keepdims=True)
    vc = v - mu
    var = jnp.mean(vc * vc, axis=-1, keepdims=True)
    gv_scr[...] = (vc * lax.rsqrt(var + EPS) * lng_ref[...] + lnb_ref[...]).astype(BF16)
    u_scr[...] = _gelu(su_ref[...].astype(F32))

    for n in range(SG_TS // CHUNK):
        rows = slice(n * CHUNK, (n + 1) * CHUNK)
        for g in range(SGU_GROUPS):
            cols = slice(g * LANES, (g + 1) * LANES)
            mixed = jnp.dot(ws_ref[g], gv_scr[rows, cols], preferred_element_type=F32) + bs_ref[g]
            sg_scr[rows, cols] = (u_scr[rows, cols] * mixed).astype(BF16)

    y = jnp.dot(sg_scr[...], wp_ref[...], preferred_element_type=F32)
    gate = jax.nn.sigmoid(gl_ref[...].astype(F32) + bg_ref[...])
    out_ref[...] = (gate * y).astype(BF16)


def _sgu_branch(z2, bg, lng, lnb, ws, bs, wp):
    def col(c):
        return pl.BlockSpec((SG_TS, D_MODEL), lambda i: (i, c))

    return pl.pallas_call(
        _sgu_kernel,
        grid=(TOKENS // SG_TS,),
        in_specs=[
            col(Z_SU // D_MODEL), col(Z_SV // D_MODEL), col(Z_GC // D_MODEL),
            _resident((1, D_MODEL)),
            _resident((1, D_MODEL)),
            _resident((1, D_MODEL)),
            _resident((SGU_GROUPS, CHUNK, CHUNK)),
            _resident((SGU_GROUPS, CHUNK, LANES)),
            _resident((D_MODEL, D_MODEL)),
        ],
        out_specs=pl.BlockSpec((SG_TS, D_MODEL), lambda i: (i, 0)),
        out_shape=jax.ShapeDtypeStruct((TOKENS, D_MODEL), BF16),
        scratch_shapes=[
            pltpu.VMEM((SG_TS, D_MODEL), F32),
            pltpu.VMEM((SG_TS, D_MODEL), BF16),
            pltpu.VMEM((SG_TS, D_MODEL), BF16),
        ],
        compiler_params=_params(("parallel",)),
        name="sgu_branch",
    )(z2, z2, z2, bg, lng, lnb, ws, bs, wp)


MG_TM = 512


def _merge_kernel(x_ref, ya_ref, o_ref, yc_ref, gl_ref, bg_ref, wa_ref, wout_ref, gpost_ref, out_ref):
    yb = jnp.dot(o_ref[...], wa_ref[...], preferred_element_type=F32)
    gate = jax.nn.sigmoid(gl_ref[...].astype(F32) + bg_ref[...])
    mix = ya_ref[...].astype(F32) + gate * yb + yc_ref[...].astype(F32)
    m = jnp.dot(mix.astype(BF16), wout_ref[...], preferred_element_type=F32)
    out_ref[...] = x_ref[...] + _rms(m, gpost_ref[...])


def _merge(x2, ya, o, yc, z2, bg, wa, wout, gpost):
    def rows(width=D_MODEL, col=0):
        return pl.BlockSpec((MG_TM, width), lambda i: (i, col))

    return pl.pallas_call(
        _merge_kernel,
        grid=(TOKENS // MG_TM,),
        in_specs=[
            rows(), rows(), rows(), rows(), rows(col=Z_GB // D_MODEL),
            _resident((1, D_MODEL)),
            _resident((D_MODEL, D_MODEL)),
            _resident((D_MODEL, D_MODEL)),
            _resident((1, D_MODEL)),
        ],
        out_specs=rows(),
        out_shape=jax.ShapeDtypeStruct((TOKENS, D_MODEL), F32),
        compiler_params=_params(("parallel",)),
        name="merge",
    )(x2, ya, o, yc, z2, bg, wa, wout, gpost)


FF_TM = 512
FF_TC = 1024


def _ffn_kernel(x_ref, gpre_ref, gpost_ref, wup_ref, wdn_ref, out_ref):
    x = x_ref[...]
    h = _rms(x, gpre_ref[...]).astype(BF16)
    acc = jnp.zeros((FF_TM, D_MODEL), F32)
    for c in range(D_FF // FF_TC):
        cols = slice(c * FF_TC, (c + 1) * FF_TC)
        u = jnp.dot(h, wup_ref[:, cols], preferred_element_type=F32)
        u = jnp.square(jnp.maximum(u, 0.0)).astype(BF16)
        acc = acc + jnp.dot(u, wdn_ref[cols, :], preferred_element_type=F32)
    out_ref[...] = x + _rms(acc, gpost_ref[...])


def _ffn(x2, gpre, gpost, wup, wdn):
    return pl.pallas_call(
        _ffn_kernel,
        grid=(TOKENS // FF_TM,),
        in_specs=[
            pl.BlockSpec((FF_TM, D_MODEL), lambda i: (i, 0)),
            _resident((1, D_MODEL)),
            _resident((1, D_MODEL)),
            _resident((D_MODEL, D_FF)),
            _resident((D_FF, D_MODEL)),
        ],
        out_specs=pl.BlockSpec((FF_TM, D_MODEL), lambda i: (i, 0)),
        out_shape=jax.ShapeDtypeStruct((TOKENS, D_MODEL), F32),
        compiler_params=_params(("parallel",)),
        name="ffn",
    )(x2, gpre, gpost, wup, wdn)


def _row(v):
    return v.reshape(1, -1).astype(F32)


def kernel(x, norm_mix_pre, norm_mix_post, w_in, b_gate, conv_w, conv_b, conv_ln_g, conv_ln_b,
           lam_q1, lam_k1, lam_q2, lam_k2, subln_g, sgu_ln_g, sgu_ln_b, sgu_w, sgu_b,
           w_proj_conv, w_proj_attn, w_proj_sgu, w_out, norm_ffn_pre, norm_ffn_post,
           w_ffn_up, w_ffn_down):
    assert x.shape == (BATCH, SEQ, D_MODEL) and w_in.shape == (DEPTH, D_MODEL, Z_COLS)
    x2 = x.reshape(TOKENS, D_MODEL).astype(F32)

    colscale = jnp.ones((1, Z_COLS), F32).at[:, Z_Q:Z_Q + D_MODEL].set(DH ** -0.5 * LOG2E)
    slopes = jnp.exp2(-8.0 * jnp.arange(1, HEADS + 1, dtype=F32) / HEADS) * LOG2E
    slopes = jnp.broadcast_to(slopes[:, None, None], (HEADS, 1, AT_TQ))

    for l in range(DEPTH):
        lam_init = 0.8 - 0.6 * math.exp(-0.3 * l)
        z2 = _proj_in(x2, _row(norm_mix_pre[l]), colscale, w_in[l].astype(BF16))
        z3 = z2.reshape(BATCH, SEQ, Z_COLS)

        cw = conv_w[l].reshape(CONV_K, CV_NC, LANES).transpose(1, 0, 2).astype(F32)
        ya = _conv_branch(z3, _row(b_gate[l, 0:D_MODEL]), cw,
                          conv_b[l].reshape(CV_NC, LANES).astype(F32),
                          _row(conv_ln_g[l]), _row(conv_ln_b[l]), w_proj_conv[l].astype(BF16))

        lam_vecs = jnp.stack([lam_q1[l], lam_q2[l], lam_k1[l], lam_k2[l]]).astype(F32)
        o = _attention(z3, lam_vecs, slopes, _row(subln_g[l]) * (1.0 - lam_init), lam_init)

        bs = jnp.broadcast_to(sgu_b[l].astype(F32)[:, :, None], (SGU_GROUPS, CHUNK, LANES))
        yc = _sgu_branch(z2, _row(b_gate[l, 2 * D_MODEL:3 * D_MODEL]), _row(sgu_ln_g[l]),
                         _row(sgu_ln_b[l]), sgu_w[l].astype(BF16), bs, w_proj_sgu[l].astype(BF16))

        x2 = _merge(x2, ya.reshape(TOKENS, D_MODEL), o.reshape(TOKENS, D_MODEL), yc, z2,
                    _row(b_gate[l, D_MODEL:2 * D_MODEL]), w_proj_attn[l].astype(BF16),
                    w_out[l].astype(BF16), _row(norm_mix_post[l]))

        x2 = _ffn(x2, _row(norm_ffn_pre[l]), _row(norm_ffn_post[l]),
                  w_ffn_up[l].astype(BF16), w_ffn_down[l].astype(BF16))

    return x2.reshape(BATCH, SEQ, D_MODEL).astype(x.dtype)
```

```python
import functools
import math

import jax
import jax.numpy as jnp
from jax import lax
from jax.experimental import pallas as pl
from jax.experimental.pallas import tpu as pltpu

F32 = jnp.float32
BF16 = jnp.bfloat16

D_MODEL = 1024
BATCH = 8
SEQ = 2048
DEPTH = 2
TOKENS = BATCH * SEQ
EPS = 1e-6

CONV_K = 31
CONV_PAD = (CONV_K - 1) // 2
HEADS = 8
DH = 64
HEAD_W = 2 * DH
SGU_GROUPS = 8
CHUNK = 128
D_FF = 4 * D_MODEL

Z_A, Z_B, Z_Q, Z_K, Z_V, Z_SU, Z_SV, Z_GA, Z_GB, Z_GC = (i * D_MODEL for i in range(10))
Z_COLS = 10 * D_MODEL

LANES = 128
SUBLANES = 8
HALO = 16
LOG2E = 1.4426950408889634

VMEM_LIMIT = 56 * 1024 * 1024


def _params(semantics):
    return pltpu.CompilerParams(dimension_semantics=semantics, vmem_limit_bytes=VMEM_LIMIT)


def _resident(shape):
    zeros = (0,) * len(shape)
    return pl.BlockSpec(shape, lambda *_: zeros, pipeline_mode=pl.Buffered(1))


def _rms(x, g):
    ms = jnp.mean(x * x, axis=-1, keepdims=True)
    return x * lax.rsqrt(ms + EPS) * g


def _sigmoid(x):
    return 0.5 * jnp.tanh(0.5 * x) + 0.5


def _gelu(x):
    return 0.5 * x * (1.0 + lax.erf(x * (2.0 ** -0.5)))


PI_TM = 2048
PI_TN = 1024
PI_RC = 256


def _proj_in_kernel(x_ref, g_ref, cs_ref, w_ref, z_ref, h_ref):
    @pl.when(pl.program_id(1) == 0)
    def _():
        for r in range(PI_TM // PI_RC):
            rows = slice(r * PI_RC, (r + 1) * PI_RC)
            h_ref[rows, :] = _rms(x_ref[rows, :], g_ref[...]).astype(BF16)

    for r in range(PI_TM // PI_RC):
        rows = slice(r * PI_RC, (r + 1) * PI_RC)
        acc = jnp.dot(h_ref[rows, :], w_ref[...], preferred_element_type=F32)
        z_ref[rows, :] = (acc * cs_ref[...]).astype(BF16)


def _proj_in(x2, g, colscale, w):
    return pl.pallas_call(
        _proj_in_kernel,
        grid=(TOKENS // PI_TM, Z_COLS // PI_TN),
        in_specs=[
            pl.BlockSpec((PI_TM, D_MODEL), lambda i, j: (i, 0)),
            _resident((1, D_MODEL)),
            pl.BlockSpec((1, PI_TN), lambda i, j: (0, j)),
            pl.BlockSpec((D_MODEL, PI_TN), lambda i, j: (0, j)),
        ],
        out_specs=pl.BlockSpec((PI_TM, PI_TN), lambda i, j: (i, j)),
        out_shape=jax.ShapeDtypeStruct((TOKENS, Z_COLS), BF16),
        scratch_shapes=[pltpu.VMEM((PI_TM, D_MODEL), BF16)],
        compiler_params=_params(("parallel", "arbitrary")),
        name="proj_in",
    )(x2, g, colscale, w)


CV_TS = 512
CV_R = 128
CV_NC = D_MODEL // LANES


def _conv_kernel(za_ref, zb_ref, zap_ref, zbp_ref, zan_ref, zbn_ref, gl_ref, bg_ref,
                 cw_ref, cb_ref, lng_ref, lnb_ref, wp_ref, out_ref, a_scr, c_scr, act_scr):
    s = pl.program_id(1)
    has_prev = s > 0
    has_next = s < pl.num_programs(1) - 1

    def glu(a, b):
        return a.astype(F32) * _sigmoid(b.astype(F32))

    for c in range(CV_NC):
        cs = slice(c * LANES, (c + 1) * LANES)
        prev = glu(zap_ref[0, :, cs], zbp_ref[0, :, cs])
        nxt = glu(zan_ref[0, :, cs], zbn_ref[0, :, cs])
        a_scr[c, 0:HALO, :] = jnp.where(has_prev, prev, 0.0)
        a_scr[c, HALO:HALO + CV_TS, :] = glu(za_ref[0, :, cs], zb_ref[0, :, cs])
        a_scr[c, HALO + CV_TS:, :] = jnp.where(has_next, nxt, 0.0)

    def chunk_body(c, carry):
        for r in range(CV_TS // CV_R):
            acc = jnp.zeros((CV_R, LANES), F32)
            for t in range(CONV_K):
                start = r * CV_R + HALO - CONV_PAD + t
                acc = acc + cw_ref[c, pl.ds(t, 1), :] * a_scr[c, pl.ds(start, CV_R), :]
            c_scr[c, r * CV_R:(r + 1) * CV_R, :] = acc + cb_ref[pl.ds(c, 1), :]
        return carry

    lax.fori_loop(0, CV_NC, chunk_body, 0)

    tot = c_scr[0]
    for c in range(1, CV_NC):
        tot = tot + c_scr[c]
    mu = jnp.sum(tot, axis=-1, keepdims=True) * (1.0 / D_MODEL)
    sq = jnp.zeros((CV_TS, LANES), F32)
    for c in range(CV_NC):
        d = c_scr[c] - mu
        sq = sq + d * d
    rstd = lax.rsqrt(jnp.sum(sq, axis=-1, keepdims=True) * (1.0 / D_MODEL) + EPS)
    for c in range(CV_NC):
        cs = slice(c * LANES, (c + 1) * LANES)
        y = (c_scr[c] - mu) * rstd * lng_ref[:, cs] + lnb_ref[:, cs]
        act_scr[:, cs] = (y * _sigmoid(y)).astype(BF16)

    y = jnp.dot(act_scr[...], wp_ref[...], preferred_element_type=F32)
    gate = _sigmoid(gl_ref[0].astype(F32) + bg_ref[...])
    out_ref[0] = (gate * y).astype(BF16)


def _conv_branch(z3, bg, cw, cb, lng, lnb, wp):
    nblk = CV_TS // HALO
    last = SEQ // HALO - 1

    def cur(col):
        return pl.BlockSpec((1, CV_TS, D_MODEL), lambda b, s: (b, s, col))

    def prev(col):
        return pl.BlockSpec((1, HALO, D_MODEL), lambda b, s: (b, jnp.maximum(s * nblk - 1, 0), col))

    def nxt(col):
        return pl.BlockSpec((1, HALO, D_MODEL), lambda b, s: (b, jnp.minimum((s + 1) * nblk, last), col))

    ca, cbk, cg = Z_A // D_MODEL, Z_B // D_MODEL, Z_GA // D_MODEL
    return pl.pallas_call(
        _conv_kernel,
        grid=(BATCH, SEQ // CV_TS),
        in_specs=[
            cur(ca), cur(cbk), prev(ca), prev(cbk), nxt(ca), nxt(cbk), cur(cg),
            _resident((1, D_MODEL)),
            _resident((CV_NC, CONV_K, LANES)),
            _resident((CV_NC, LANES)),
            _resident((1, D_MODEL)),
            _resident((1, D_MODEL)),
            _resident((D_MODEL, D_MODEL)),
        ],
        out_specs=pl.BlockSpec((1, CV_TS, D_MODEL), lambda b, s: (b, s, 0)),
        out_shape=jax.ShapeDtypeStruct((BATCH, SEQ, D_MODEL), BF16),
        scratch_shapes=[
            pltpu.VMEM((CV_NC, CV_TS + 2 * HALO, LANES), F32),
            pltpu.VMEM((CV_NC, CV_TS, LANES), F32),
            pltpu.VMEM((CV_TS, D_MODEL), BF16),
        ],
        compiler_params=_params(("parallel", "arbitrary")),
        name="conv_branch",
    )(z3, z3, z3, z3, z3, z3, z3, bg, cw, cb, lng, lnb, wp)


AT_TQ = 256
AT_NB = 2
AT_RB = 64
AT_KT = 256
AT_NKB = SEQ // AT_KT
AT_NKB_LOG2 = AT_NKB.bit_length() - 1
AT_BLK_SCR = 7
assert AT_NKB == 1 << AT_NKB_LOG2 and AT_KT == AT_TQ


def _score_softmax_pass(score_jobs, softmax_jobs, corr_ref):
    accs = [None] * len(score_jobs)
    sums = [jnp.zeros((SUBLANES, AT_TQ), F32)] * len(softmax_jobs)
    tiled = (AT_RB // SUBLANES, SUBLANES, AT_TQ)
    for r in range(SEQ // AT_RB):
        rows = slice(r * AT_RB, (r + 1) * AT_RB)
        popped = []
        for i, (s, s_ref) in enumerate(score_jobs):
            tile = s[rows, :]
            if r < AT_KT // AT_RB:
                tile = tile + corr_ref[rows, :]
            s_ref[rows, :] = tile
            part = jnp.max(tile.reshape(tiled), axis=0)
            accs[i] = part if accs[i] is None else jnp.maximum(accs[i], part)
            popped.append(tile[0:SUBLANES, :])
        for i, (s_ref, m, p_ref) in enumerate(softmax_jobs):
            if popped:
                m = m + _zero_of(popped[i])
            p = jnp.exp2(s_ref[rows, :].reshape(tiled) - m[None])
            sums[i] = sums[i] + jnp.sum(p, axis=0)
            p_ref[rows, :] = p.reshape(AT_RB, AT_TQ).astype(BF16)
    maxes = [jnp.broadcast_to(jnp.max(a, axis=0, keepdims=True), (SUBLANES, AT_TQ)) for a in accs]
    return maxes, [jnp.sum(x, axis=0, keepdims=True) for x in sums]


def _zero_of(x):
    bits = pltpu.bitcast(x, jnp.uint32)
    return pltpu.bitcast((bits >> 16) >> 16, F32)


def _attn_kernel(lam_init, lam_ref, slope_ref, kc_ref, qc_ref, q_ref, k_ref, v_ref, g_ref, o_ref,
                 kaug_scr, qbase_scr, corr_scr, klr0_scr, klr1_scr, vt_scr, *per_block):
    klr_scr = (klr0_scr, klr1_scr)
    blk_scr = [per_block[AT_BLK_SCR * n:AT_BLK_SCR * (n + 1)] for n in range(AT_NB)]
    b = pl.program_id(1)
    step = pl.program_id(2)
    lane1 = lax.broadcasted_iota(jnp.int32, (1, HEAD_W), 1) & (DH - 1)

    @pl.when((b == 0) & (step == 0))
    def _():
        is_jl = ((lane1 >= 6) & (lane1 < 9)).astype(F32)
        is_jh = ((lane1 >= 9) & (lane1 < 12)).astype(F32)
        is_il = (lane1 < 3).astype(F32)
        row = lax.broadcasted_iota(jnp.int32, (SEQ, HEAD_W), 0)
        jl = (row & (AT_TQ - 1)).astype(F32)
        jh = (row - (row & (AT_TQ - 1))).astype(F32)
        kaug_scr[...] = kc_ref[0] + is_jl * jl + is_jh * jh
        il = lax.broadcasted_iota(jnp.int32, (AT_TQ, HEAD_W), 0).astype(F32)
        qbase_scr[...] = qc_ref[0] + is_il * il
        kk = lax.broadcasted_iota(jnp.int32, (AT_TQ, AT_TQ), 0)
        qq = lax.broadcasted_iota(jnp.int32, (AT_TQ, AT_TQ), 1)
        corr_scr[...] = -2.0 * slope_ref[0] * jnp.maximum(kk - qq, 0).astype(F32)

    @pl.when(step == 0)
    def _():
        half_k = lax.broadcasted_iota(jnp.int32, (SEQ, HEAD_W), 1) >> 6
        kf = k_ref[0].astype(F32)
        kaug = kaug_scr[...]
        for mp in range(2):
            klr_scr[mp][0] = jnp.where(half_k == mp, kf, kaug).astype(BF16)
            klr_scr[mp][1] = jnp.where(half_k == mp, kf, -kaug).astype(BF16)
        for j in range(AT_NKB):
            vt_scr[j] = v_ref[0, j * AT_KT:(j + 1) * AT_KT, :].astype(F32).T.astype(BF16)

    lam_v = lam_ref[...]
    dots = jnp.sum(lam_v[0:2, :] * lam_v[2:4, :], axis=-1, keepdims=True)
    e = jnp.exp(dots)
    lam = e[0:1, :] - e[1:2, :] + lam_init

    is_q0 = ((lane1 >= 3) & (lane1 < 6)).astype(F32)
    half_q = lax.broadcasted_iota(jnp.int32, (AT_TQ, HEAD_W), 1) >> 6

    def qk_stage(n):
        km_scr, vtr_scr, s_scr = blk_scr[n][0:2], blk_scr[n][2], blk_scr[n][3:5]
        qi = step * AT_NB + n
        for t in range(AT_NKB):
            kblk = (qi + t) & (AT_NKB - 1)
            side = 0 if t == 0 else 1 - ((qi + t) >> AT_NKB_LOG2)
            src = pl.ds(pl.multiple_of(kblk * AT_KT, AT_KT), AT_KT)
            for mp in range(2):
                km_scr[mp][t * AT_KT:(t + 1) * AT_KT, :] = klr_scr[mp][side, src, :]
            vtr_scr[:, t * AT_KT:(t + 1) * AT_KT] = vt_scr[kblk]
        qf = q_ref[0, n * AT_TQ:(n + 1) * AT_TQ, :].astype(F32)
        qaug = qbase_scr[...] + is_q0 * (qi * AT_TQ).astype(F32)
        jobs = []
        for mp in range(2):
            qa = jnp.where(half_q == mp, qf, qaug).astype(BF16)
            s = lax.dot_general(km_scr[mp][...], qa, (((1,), (1,)), ((), ())),
                                preferred_element_type=F32)
            jobs.append((s, s_scr[mp]))
        return jobs

    def softmax_jobs(n, mx):
        s_scr, p_scr = blk_scr[n][3:5], blk_scr[n][5:7]
        return [(s_scr[mp], mx[mp], p_scr[mp]) for mp in range(2)]

    def pv_stage(n, denom):
        vtr_scr, p_scr = blk_scr[n][2], blk_scr[n][5:7]
        outs = [jnp.dot(vtr_scr[...], p_scr[mp][...], preferred_element_type=F32) / denom[mp]
                for mp in range(2)]
        ot = outs[0] - lam * outs[1]
        ms = jnp.mean(ot * ot, axis=0, keepdims=True)
        ot = ot * lax.rsqrt(ms + EPS)
        o_ref[0, n * AT_TQ:(n + 1) * AT_TQ, :] = (ot.T * g_ref[...]).astype(BF16)

    mx, _ = _score_softmax_pass(qk_stage(0), [], corr_scr)
    for n in range(AT_NB):
        nxt = qk_stage(n + 1) if n + 1 < AT_NB else []
        mx, denom = _score_softmax_pass(nxt, softmax_jobs(n, mx), corr_scr)
        pv_stage(n, denom)


def _attention(z3, lam_vecs, slopes, kconst, qconst, gain, lam_init):
    qc, kc, vc = Z_Q // HEAD_W, Z_K // HEAD_W, Z_V // HEAD_W
    return pl.pallas_call(
        functools.partial(_attn_kernel, lam_init),
        grid=(HEADS, BATCH, SEQ // (AT_TQ * AT_NB)),
        in_specs=[
            _resident((4, DH)),
            pl.BlockSpec((1, 1, AT_TQ), lambda h, b, i: (h, 0, 0)),
            pl.BlockSpec((1, 1, HEAD_W), lambda h, b, i: (h, 0, 0)),
            pl.BlockSpec((1, 1, HEAD_W), lambda h, b, i: (h, 0, 0)),
            pl.BlockSpec((1, AT_TQ * AT_NB, HEAD_W), lambda h, b, i: (b, i, qc + h)),
            pl.BlockSpec((1, SEQ, HEAD_W), lambda h, b, i: (b, 0, kc + h)),
            pl.BlockSpec((1, SEQ, HEAD_W), lambda h, b, i: (b, 0, vc + h)),
            _resident((1, HEAD_W)),
        ],
        out_specs=pl.BlockSpec((1, AT_TQ * AT_NB, HEAD_W), lambda h, b, i: (b, i, h)),
        out_shape=jax.ShapeDtypeStruct((BATCH, SEQ, HEADS * HEAD_W), BF16),
        scratch_shapes=[
            pltpu.VMEM((SEQ, HEAD_W), F32),
            pltpu.VMEM((AT_TQ, HEAD_W), F32),
            pltpu.VMEM((AT_TQ, AT_TQ), F32),
            pltpu.VMEM((2, SEQ, HEAD_W), BF16),
            pltpu.VMEM((2, SEQ, HEAD_W), BF16),
            pltpu.VMEM((AT_NKB, HEAD_W, AT_KT), BF16),
        ] + AT_NB * [
            pltpu.VMEM((SEQ, HEAD_W), BF16),
            pltpu.VMEM((SEQ, HEAD_W), BF16),
            pltpu.VMEM((HEAD_W, SEQ), BF16),
            pltpu.VMEM((SEQ, AT_TQ), F32),
            pltpu.VMEM((SEQ, AT_TQ), F32),
            pltpu.VMEM((SEQ, AT_TQ), BF16),
            pltpu.VMEM((SEQ, AT_TQ), BF16),
        ],
        compiler_params=_params(("arbitrary", "arbitrary", "arbitrary")),
        name="diff_attention",
    )(lam_vecs, slopes, kconst, qconst, z3, z3, z3, gain)


SG_TS = 256


def _sgu_kernel(su_ref, sv_ref, gl_ref, bg_ref, lng_ref, lnb_ref, ws_ref, bs_ref, wp_ref,
                out_ref, u_scr, gv_scr, sg_scr):
    v = _gelu(sv_ref[...].astype(F32))
    mu = jnp.mean(v, axis=-1, keepdims=True)
    vc = v - mu
    var = jnp.mean(vc * vc, axis=-1, keepdims=True)
    gv_scr[...] = (vc * lax.rsqrt(var + EPS) * lng_ref[...] + lnb_ref[...]).astype(BF16)
    u_scr[...] = _gelu(su_ref[...].astype(F32))

    for n in range(SG_TS // CHUNK):
        rows = slice(n * CHUNK, (n + 1) * CHUNK)
        for g in range(SGU_GROUPS):
            cols = slice(g * LANES, (g + 1) * LANES)
            mixed = jnp.dot(ws_ref[g], gv_scr[rows, cols], preferred_element_type=F32) + bs_ref[g]
            sg_scr[rows, cols] = (u_scr[rows, cols] * mixed).astype(BF16)

    y = jnp.dot(sg_scr[...], wp_ref[...], preferred_element_type=F32)
    gate = _sigmoid(gl_ref[...].astype(F32) + bg_ref[...])
    out_ref[...] = (gate * y).astype(BF16)


def _sgu_branch(z2, bg, lng, lnb, ws, bs, wp):
    def col(c):
        return pl.BlockSpec((SG_TS, D_MODEL), lambda i: (i, c))

    return pl.pallas_call(
        _sgu_kernel,
        grid=(TOKENS // SG_TS,),
        in_specs=[
            col(Z_SU // D_MODEL), col(Z_SV // D_MODEL), col(Z_GC // D_MODEL),
            _resident((1, D_MODEL)),
            _resident((1, D_MODEL)),
            _resident((1, D_MODEL)),
            _resident((SGU_GROUPS, CHUNK, CHUNK)),
            _resident((SGU_GROUPS, CHUNK, LANES)),
            _resident((D_MODEL, D_MODEL)),
        ],
        out_specs=pl.BlockSpec((SG_TS, D_MODEL), lambda i: (i, 0)),
        out_shape=jax.ShapeDtypeStruct((TOKENS, D_MODEL), BF16),
        scratch_shapes=[
            pltpu.VMEM((SG_TS, D_MODEL), F32),
            pltpu.VMEM((SG_TS, D_MODEL), BF16),
            pltpu.VMEM((SG_TS, D_MODEL), BF16),
        ],
        compiler_params=_params(("parallel",)),
        name="sgu_branch",
    )(z2, z2, z2, bg, lng, lnb, ws, bs, wp)


MG_TM = 512


def _merge_kernel(x_ref, ya_ref, o_ref, yc_ref, gl_ref, bg_ref, wa_ref, wout_ref, gpost_ref, out_ref):
    yb = jnp.dot(o_ref[...], wa_ref[...], preferred_element_type=F32)
    gate = _sigmoid(gl_ref[...].astype(F32) + bg_ref[...])
    mix = ya_ref[...].astype(F32) + gate * yb + yc_ref[...].astype(F32)
    m = jnp.dot(mix.astype(BF16), wout_ref[...], preferred_element_type=F32)
    out_ref[...] = x_ref[...] + _rms(m, gpost_ref[...])


def _merge(x2, ya, o, yc, z2, bg, wa, wout, gpost):
    def rows(width=D_MODEL, col=0):
        return pl.BlockSpec((MG_TM, width), lambda i: (i, col))

    return pl.pallas_call(
        _merge_kernel,
        grid=(TOKENS // MG_TM,),
        in_specs=[
            rows(), rows(), rows(), rows(), rows(col=Z_GB // D_MODEL),
            _resident((1, D_MODEL)),
            _resident((D_MODEL, D_MODEL)),
            _resident((D_MODEL, D_MODEL)),
            _resident((1, D_MODEL)),
        ],
        out_specs=rows(),
        out_shape=jax.ShapeDtypeStruct((TOKENS, D_MODEL), F32),
        compiler_params=_params(("parallel",)),
        name="merge",
    )(x2, ya, o, yc, z2, bg, wa, wout, gpost)


FF_TM = 512
FF_TC = 1024


def _ffn_kernel(x_ref, gpre_ref, gpost_ref, wup_ref, wdn_ref, out_ref):
    x = x_ref[...]
    h = _rms(x, gpre_ref[...]).astype(BF16)
    acc = jnp.zeros((FF_TM, D_MODEL), F32)
    for c in range(D_FF // FF_TC):
        cols = slice(c * FF_TC, (c + 1) * FF_TC)
        u = jnp.dot(h, wup_ref[:, cols], preferred_element_type=F32)
        u = jnp.square(jnp.maximum(u, 0.0)).astype(BF16)
        acc = acc + jnp.dot(u, wdn_ref[cols, :], preferred_element_type=F32)
    out_ref[...] = x + _rms(acc, gpost_ref[...])


def _ffn(x2, gpre, gpost, wup, wdn):
    return pl.pallas_call(
        _ffn_kernel,
        grid=(TOKENS // FF_TM,),
        in_specs=[
            pl.BlockSpec((FF_TM, D_MODEL), lambda i: (i, 0)),
            _resident((1, D_MODEL)),
            _resident((1, D_MODEL)),
            _resident((D_MODEL, D_FF)),
            _resident((D_FF, D_MODEL)),
        ],
        out_specs=pl.BlockSpec((FF_TM, D_MODEL), lambda i: (i, 0)),
        out_shape=jax.ShapeDtypeStruct((TOKENS, D_MODEL), F32),
        compiler_params=_params(("parallel",)),
        name="ffn",
    )(x2, gpre, gpost, wup, wdn)


def _row(v):
    return v.reshape(1, -1).astype(F32)


def kernel(x, norm_mix_pre, norm_mix_post, w_in, b_gate, conv_w, conv_b, conv_ln_g, conv_ln_b,
           lam_q1, lam_k1, lam_q2, lam_k2, subln_g, sgu_ln_g, sgu_ln_b, sgu_w, sgu_b,
           w_proj_conv, w_proj_attn, w_proj_sgu, w_out, norm_ffn_pre, norm_ffn_post,
           w_ffn_up, w_ffn_down):
    assert x.shape == (BATCH, SEQ, D_MODEL) and w_in.shape == (DEPTH, D_MODEL, Z_COLS)
    x2 = x.reshape(TOKENS, D_MODEL).astype(F32)

    colscale = jnp.ones((1, Z_COLS), F32).at[:, Z_Q:Z_Q + D_MODEL].set(DH ** -0.5 * LOG2E)
    slopes = jnp.exp2(-8.0 * jnp.arange(1, HEADS + 1, dtype=F32) / HEADS) * LOG2E
    c1 = slopes.astype(BF16).astype(F32)
    c2 = (slopes - c1).astype(BF16).astype(F32)
    c3 = (slopes - c1 - c2).astype(BF16).astype(F32)
    pieces = jnp.stack([c1, c2, c3], axis=1)
    a = jnp.arange(HEAD_W) & (DH - 1)
    kconst = jnp.where(a < 6, -pieces[:, a % 3], 0.0)[:, None, :]
    qconst = jnp.where((a >= 6) & (a < 12), pieces[:, a % 3], 0.0)[:, None, :]
    slopes = jnp.broadcast_to(slopes[:, None, None], (HEADS, 1, AT_TQ))

    for l in range(DEPTH):
        lam_init = 0.8 - 0.6 * math.exp(-0.3 * l)
        z2 = _proj_in(x2, _row(norm_mix_pre[l]), colscale, w_in[l].astype(BF16))
        z3 = z2.reshape(BATCH, SEQ, Z_COLS)

        cw = conv_w[l].reshape(CONV_K, CV_NC, LANES).transpose(1, 0, 2).astype(F32)
        ya = _conv_branch(z3, _row(b_gate[l, 0:D_MODEL]), cw,
                          conv_b[l].reshape(CV_NC, LANES).astype(F32),
                          _row(conv_ln_g[l]), _row(conv_ln_b[l]), w_proj_conv[l].astype(BF16))

        lam_vecs = jnp.stack([lam_q1[l], lam_q2[l], lam_k1[l], lam_k2[l]]).astype(F32)
        o = _attention(z3, lam_vecs, slopes, kconst, qconst,
                       _row(subln_g[l]) * (1.0 - lam_init), lam_init)

        bs = jnp.broadcast_to(sgu_b[l].astype(F32)[:, :, None], (SGU_GROUPS, CHUNK, LANES))
        yc = _sgu_branch(z2, _row(b_gate[l, 2 * D_MODEL:3 * D_MODEL]), _row(sgu_ln_g[l]),
                         _row(sgu_ln_b[l]), sgu_w[l].astype(BF16), bs, w_proj_sgu[l].astype(BF16))

        x2 = _merge(x2, ya.reshape(TOKENS, D_MODEL), o.reshape(TOKENS, D_MODEL), yc, z2,
                    _row(b_gate[l, D_MODEL:2 * D_MODEL]), w_proj_attn[l].astype(BF16),
                    w_out[l].astype(BF16), _row(norm_mix_post[l]))

        x2 = _ffn(x2, _row(norm_ffn_pre[l]), _row(norm_ffn_post[l]),
                  w_ffn_up[l].astype(BF16), w_ffn_down[l].astype(BF16))

    return x2.reshape(BATCH, SEQ, D_MODEL).astype(x.dtype)
```

```python
import functools
import math

import jax
import jax.numpy as jnp
from jax import lax
from jax.experimental import pallas as pl
from jax.experimental.pallas import tpu as pltpu

F32 = jnp.float32
BF16 = jnp.bfloat16

D_MODEL = 1024
BATCH = 8
SEQ = 2048
DEPTH = 2
TOKENS = BATCH * SEQ
EPS = 1e-6

CONV_K = 31
CONV_PAD = (CONV_K - 1) // 2
HEADS = 8
DH = 64
HEAD_W = 2 * DH
SGU_GROUPS = 8
CHUNK = 128
D_FF = 4 * D_MODEL

Z_A, Z_B, Z_Q, Z_K, Z_V, Z_SU, Z_SV, Z_GA, Z_GB, Z_GC = (i * D_MODEL for i in range(10))
Z_COLS = 10 * D_MODEL

LANES = 128
SUBLANES = 8
HALO = 16
LOG2E = 1.4426950408889634

VMEM_LIMIT = 56 * 1024 * 1024


def _params(semantics):
    return pltpu.CompilerParams(dimension_semantics=semantics, vmem_limit_bytes=VMEM_LIMIT)


def _resident(shape):
    zeros = (0,) * len(shape)
    return pl.BlockSpec(shape, lambda *_: zeros, pipeline_mode=pl.Buffered(1))


def _rms(x, g):
    ms = jnp.mean(x * x, axis=-1, keepdims=True)
    return x * lax.rsqrt(ms + EPS) * g


def _sigmoid(x):
    return 0.5 * jnp.tanh(0.5 * x) + 0.5


def _gelu(x):
    return 0.5 * x * (1.0 + lax.erf(x * (2.0 ** -0.5)))


PI_TM = SEQ
PI_TN = 1024
PI_RC = 256
PI_CONV_J0 = 2
CV_R = 128
CV_NC = D_MODEL // LANES
assert Z_A == 0 and Z_B == PI_TN and Z_COLS // PI_TN == PI_CONV_J0 + CV_NC


def _proj_in_kernel(x_ref, g_ref, cs_ref, w_ref, cw_ref, cb_ref, z_ref, cv_ref, h_ref, zab_scr, a_scr):
    j = pl.program_id(1)

    @pl.when(j == 0)
    def _():
        for r in range(PI_TM // PI_RC):
            rows = slice(r * PI_RC, (r + 1) * PI_RC)
            h_ref[rows, :] = _rms(x_ref[rows, :], g_ref[...]).astype(BF16)

    def project(r, keep):
        rows = slice(r * PI_RC, (r + 1) * PI_RC)
        acc = jnp.dot(h_ref[rows, :], w_ref[...], preferred_element_type=F32)
        zb = (acc * cs_ref[...]).astype(BF16)
        z_ref[rows, :] = zb
        if keep:
            for c in range(CV_NC):
                zab_scr[j, c, rows, :] = zb[:, c * LANES:(c + 1) * LANES]

    @pl.when(j < PI_CONV_J0)
    def _():
        for r in range(PI_TM // PI_RC):
            project(r, keep=True)

    @pl.when(j >= PI_CONV_J0)
    def _():
        c = j - PI_CONV_J0
        zero = jnp.zeros((HALO, LANES), F32)
        a_scr[0:HALO, :] = zero
        a_scr[HALO + SEQ:, :] = zero
        for r in range(SEQ // PI_RC):
            rows = slice(r * PI_RC, (r + 1) * PI_RC)
            glu = zab_scr[0, c, rows, :].astype(F32) * _sigmoid(zab_scr[1, c, rows, :].astype(F32))
            a_scr[HALO + r * PI_RC:HALO + (r + 1) * PI_RC, :] = glu
        for r in range(PI_TM // PI_RC):
            project(r, keep=False)
            for rr in range(r * (PI_RC // CV_R), (r + 1) * (PI_RC // CV_R)):
                acc = jnp.zeros((CV_R, LANES), F32)
                for t in range(CONV_K):
                    start = rr * CV_R + HALO - CONV_PAD + t
                    acc = acc + cw_ref[0, t:t + 1, :] * a_scr[start:start + CV_R, :]
                cv_ref[rr * CV_R:(rr + 1) * CV_R, :] = acc + cb_ref[0]


def _proj_in(x2, g, colscale, w, cw, cb):
    def chunk(i, j):
        return jnp.maximum(j - PI_CONV_J0, 0)

    return pl.pallas_call(
        _proj_in_kernel,
        grid=(TOKENS // PI_TM, Z_COLS // PI_TN),
        in_specs=[
            pl.BlockSpec((PI_TM, D_MODEL), lambda i, j: (i, 0)),
            _resident((1, D_MODEL)),
            pl.BlockSpec((1, PI_TN), lambda i, j: (0, j)),
            pl.BlockSpec((D_MODEL, PI_TN), lambda i, j: (0, j)),
            pl.BlockSpec((1, CONV_K, LANES), lambda i, j: (chunk(i, j), 0, 0)),
            pl.BlockSpec((1, 1, LANES), lambda i, j: (chunk(i, j), 0, 0)),
        ],
        out_specs=[
            pl.BlockSpec((PI_TM, PI_TN), lambda i, j: (i, j)),
            pl.BlockSpec((PI_TM, LANES), lambda i, j: (i, chunk(i, j))),
        ],
        out_shape=[
            jax.ShapeDtypeStruct((TOKENS, Z_COLS), BF16),
            jax.ShapeDtypeStruct((TOKENS, D_MODEL), F32),
        ],
        scratch_shapes=[
            pltpu.VMEM((PI_TM, D_MODEL), BF16),
            pltpu.VMEM((PI_CONV_J0, CV_NC, SEQ, LANES), BF16),
            pltpu.VMEM((SEQ + 2 * HALO, LANES), F32),
        ],
        compiler_params=_params(("parallel", "arbitrary")),
        name="proj_in",
    )(x2, g, colscale, w, cw, cb)


CV_TS = 512
CV_RC = 128


def _conv_kernel(cv_ref, gl_ref, bg_ref, lng_ref, lnb_ref, wp_ref, out_ref, act_scr):
    for r in range(CV_TS // CV_RC):
        rows = slice(r * CV_RC, (r + 1) * CV_RC)
        c = cv_ref[rows, :]
        mu = jnp.mean(c, axis=-1, keepdims=True)
        d = c - mu
        var = jnp.mean(d * d, axis=-1, keepdims=True)
        y = d * lax.rsqrt(var + EPS) * lng_ref[...] + lnb_ref[...]
        act_scr[rows, :] = (y * _sigmoid(y)).astype(BF16)

    y = jnp.dot(act_scr[...], wp_ref[...], preferred_element_type=F32)
    gate = _sigmoid(gl_ref[...].astype(F32) + bg_ref[...])
    out_ref[...] = (gate * y).astype(BF16)


def _conv_branch(conv_raw, z2, bg, lng, lnb, wp):
    return pl.pallas_call(
        _conv_kernel,
        grid=(TOKENS // CV_TS,),
        in_specs=[
            pl.BlockSpec((CV_TS, D_MODEL), lambda i: (i, 0)),
            pl.BlockSpec((CV_TS, D_MODEL), lambda i: (i, Z_GA // D_MODEL)),
            _resident((1, D_MODEL)),
            _resident((1, D_MODEL)),
            _resident((1, D_MODEL)),
            _resident((D_MODEL, D_MODEL)),
        ],
        out_specs=pl.BlockSpec((CV_TS, D_MODEL), lambda i: (i, 0)),
        out_shape=jax.ShapeDtypeStruct((TOKENS, D_MODEL), BF16),
        scratch_shapes=[pltpu.VMEM((CV_TS, D_MODEL), BF16)],
        compiler_params=_params(("parallel",)),
        name="conv_branch",
    )(conv_raw, z2, bg, lng, lnb, wp)


AT_TQ = 256
AT_NB = 4
AT_RB = 64
AT_KT = 256
AT_NKB = SEQ // AT_KT
AT_NKB_LOG2 = AT_NKB.bit_length() - 1
AT_BLK_SCR = 5
assert AT_NKB == 1 << AT_NKB_LOG2 and AT_KT == AT_TQ


def _score_softmax_pass(score_jobs, softmax_jobs, corr_ref):
    accs = [None] * len(score_jobs)
    sums = [jnp.zeros((SUBLANES, AT_TQ), F32)] * len(softmax_jobs)
    tiled = (AT_RB // SUBLANES, SUBLANES, AT_TQ)
    for t in range(AT_NKB):
        for i, (qk, s_ref) in enumerate(score_jobs):
            s = qk(t)
            for r in range(AT_KT // AT_RB):
                tile = s[r * AT_RB:(r + 1) * AT_RB, :]
                if t == 0:
                    tile = tile + corr_ref[r * AT_RB:(r + 1) * AT_RB, :]
                s_ref[t * AT_KT + r * AT_RB:t * AT_KT + (r + 1) * AT_RB, :] = tile
                part = jnp.max(tile.reshape(tiled), axis=0)
                accs[i] = part if accs[i] is None else jnp.maximum(accs[i], part)
        for i, (s_ref, m, p_ref) in enumerate(softmax_jobs):
            for r in range(AT_KT // AT_RB):
                rows = slice(t * AT_KT + r * AT_RB, t * AT_KT + (r + 1) * AT_RB)
                p = jnp.exp2(s_ref[rows, :].reshape(tiled) - m[None])
                sums[i] = sums[i] + jnp.sum(p, axis=0)
                p_ref[rows, :] = p.reshape(AT_RB, AT_TQ).astype(BF16)
    maxes = [jnp.broadcast_to(jnp.max(a, axis=0, keepdims=True), (SUBLANES, AT_TQ)) for a in accs]
    return maxes, [jnp.sum(x, axis=0, keepdims=True) for x in sums]


def _attn_kernel(lam_init, lam_ref, slope_ref, kc_ref, qc_ref, q_ref, k_ref, v_ref, g_ref, o_ref,
                 kaug_scr, qbase_scr, corr_scr, klr0_scr, klr1_scr, vt_scr, *per_block):
    klr_scr = (klr0_scr, klr1_scr)
    blk_scr = [per_block[AT_BLK_SCR * n:AT_BLK_SCR * (n + 1)] for n in range(AT_NB)]
    b = pl.program_id(1)
    step = pl.program_id(2)
    lane1 = lax.broadcasted_iota(jnp.int32, (1, HEAD_W), 1) & (DH - 1)

    @pl.when((b == 0) & (step == 0))
    def _():
        is_jl = ((lane1 >= 6) & (lane1 < 9)).astype(F32)
        is_jh = ((lane1 >= 9) & (lane1 < 12)).astype(F32)
        is_il = (lane1 < 3).astype(F32)
        row = lax.broadcasted_iota(jnp.int32, (SEQ, HEAD_W), 0)
        jl = (row & (AT_TQ - 1)).astype(F32)
        jh = (row - (row & (AT_TQ - 1))).astype(F32)
        kaug_scr[...] = kc_ref[0] + is_jl * jl + is_jh * jh
        il = lax.broadcasted_iota(jnp.int32, (AT_TQ, HEAD_W), 0).astype(F32)
        qbase_scr[...] = qc_ref[0] + is_il * il
        kk = lax.broadcasted_iota(jnp.int32, (AT_TQ, AT_TQ), 0)
        qq = lax.broadcasted_iota(jnp.int32, (AT_TQ, AT_TQ), 1)
        corr_scr[...] = -2.0 * slope_ref[0] * jnp.maximum(kk - qq, 0).astype(F32)

    @pl.when(step == 0)
    def _():
        half_k = lax.broadcasted_iota(jnp.int32, (SEQ, HEAD_W), 1) >> 6
        kf = k_ref[0].astype(F32)
        kaug = kaug_scr[...]
        for mp in range(2):
            klr_scr[mp][0] = jnp.where(half_k == mp, kf, kaug).astype(BF16)
            klr_scr[mp][1] = jnp.where(half_k == mp, kf, -kaug).astype(BF16)
        for j in range(AT_NKB):
            vt_scr[j] = v_ref[0, j * AT_KT:(j + 1) * AT_KT, :].astype(F32).T.astype(BF16)

    lam_v = lam_ref[...]
    dots = jnp.sum(lam_v[0:2, :] * lam_v[2:4, :], axis=-1, keepdims=True)
    e = jnp.exp(dots)
    lam = e[0:1, :] - e[1:2, :] + lam_init

    is_q0 = ((lane1 >= 3) & (lane1 < 6)).astype(F32)
    half_q = lax.broadcasted_iota(jnp.int32, (AT_TQ, HEAD_W), 1) >> 6

    def qk_stage(n):
        vtr_scr, s_scr = blk_scr[n][0], blk_scr[n][1:3]
        qi = step * AT_NB + n
        kblk = [(qi + t) & (AT_NKB - 1) for t in range(AT_NKB)]
        for t in range(AT_NKB):
            vtr_scr[:, t * AT_KT:(t + 1) * AT_KT] = vt_scr[kblk[t]]
        qf = q_ref[0, n * AT_TQ:(n + 1) * AT_TQ, :].astype(F32)
        qaug = qbase_scr[...] + is_q0 * (qi * AT_TQ).astype(F32)
        jobs = []
        for mp in range(2):
            qat = jnp.where(half_q == mp, qf, qaug).T.astype(BF16)

            def qk(t, mp=mp, qat=qat):
                side = 0 if t == 0 else 1 - ((qi + t) >> AT_NKB_LOG2)
                src = pl.ds(pl.multiple_of(kblk[t] * AT_KT, AT_KT), AT_KT)
                return jnp.dot(klr_scr[mp][side, src, :], qat, preferred_element_type=F32)

            jobs.append((qk, s_scr[mp]))
        return jobs

    def softmax_jobs(n, mx):
        s_scr, p_scr = blk_scr[n][1:3], blk_scr[n][3:5]
        return [(s_scr[mp], mx[mp], p_scr[mp]) for mp in range(2)]

    def pv_stage(n, denom):
        vtr_scr, p_scr = blk_scr[n][0], blk_scr[n][3:5]
        outs = [jnp.dot(vtr_scr[...], p_scr[mp][...], preferred_element_type=F32) / denom[mp]
                for mp in range(2)]
        ot = outs[0] - lam * outs[1]
        ms = jnp.mean(ot * ot, axis=0, keepdims=True)
        ot = ot * lax.rsqrt(ms + EPS)
        o_ref[0, n * AT_TQ:(n + 1) * AT_TQ, :] = (ot.T * g_ref[...]).astype(BF16)

    mx, _ = _score_softmax_pass(qk_stage(0), [], corr_scr)
    for n in range(AT_NB):
        nxt = qk_stage(n + 1) if n + 1 < AT_NB else []
        mx, denom = _score_softmax_pass(nxt, softmax_jobs(n, mx), corr_scr)
        pv_stage(n, denom)


def _attention(z3, lam_vecs, slopes, kconst, qconst, gain, lam_init):
    qc, kc, vc = Z_Q // HEAD_W, Z_K // HEAD_W, Z_V // HEAD_W
    return pl.pallas_call(
        functools.partial(_attn_kernel, lam_init),
        grid=(HEADS, BATCH, SEQ // (AT_TQ * AT_NB)),
        in_specs=[
            _resident((4, DH)),
            pl.BlockSpec((1, 1, AT_TQ), lambda h, b, i: (h, 0, 0)),
            pl.BlockSpec((1, 1, HEAD_W), lambda h, b, i: (h, 0, 0)),
            pl.BlockSpec((1, 1, HEAD_W), lambda h, b, i: (h, 0, 0)),
            pl.BlockSpec((1, AT_TQ * AT_NB, HEAD_W), lambda h, b, i: (b, i, qc + h)),
            pl.BlockSpec((1, SEQ, HEAD_W), lambda h, b, i: (b, 0, kc + h)),
            pl.BlockSpec((1, SEQ, HEAD_W), lambda h, b, i: (b, 0, vc + h)),
            _resident((1, HEAD_W)),
        ],
        out_specs=pl.BlockSpec((1, AT_TQ * AT_NB, HEAD_W), lambda h, b, i: (b, i, h)),
        out_shape=jax.ShapeDtypeStruct((BATCH, SEQ, HEADS * HEAD_W), BF16),
        scratch_shapes=[
            pltpu.VMEM((SEQ, HEAD_W), F32),
            pltpu.VMEM((AT_TQ, HEAD_W), F32),
            pltpu.VMEM((AT_TQ, AT_TQ), F32),
            pltpu.VMEM((2, SEQ, HEAD_W), BF16),
            pltpu.VMEM((2, SEQ, HEAD_W), BF16),
            pltpu.VMEM((AT_NKB, HEAD_W, AT_KT), BF16),
        ] + AT_NB * [
            pltpu.VMEM((HEAD_W, SEQ), BF16),
            pltpu.VMEM((SEQ, AT_TQ), F32),
            pltpu.VMEM((SEQ, AT_TQ), F32),
            pltpu.VMEM((SEQ, AT_TQ), BF16),
            pltpu.VMEM((SEQ, AT_TQ), BF16),
        ],
        compiler_params=_params(("arbitrary", "arbitrary", "arbitrary")),
        name="diff_attention",
    )(lam_vecs, slopes, kconst, qconst, z3, z3, z3, gain)


SG_TS = 256


def _sgu_kernel(su_ref, sv_ref, gl_ref, bg_ref, lng_ref, lnb_ref, ws_ref, bs_ref, wp_ref,
                out_ref, u_scr, gv_scr, sg_scr):
    v = _gelu(sv_ref[...].astype(F32))
    mu = jnp.mean(v, axis=-1, keepdims=True)
    vc = v - mu
    var = jnp.mean(vc * vc, axis=-1, keepdims=True)
    gv_scr[...] = (vc * lax.rsqrt(var + EPS) * lng_ref[...] + lnb_ref[...]).astype(BF16)
    u_scr[...] = _gelu(su_ref[...].astype(F32))

    for n in range(SG_TS // CHUNK):
        rows = slice(n * CHUNK, (n + 1) * CHUNK)
        for g in range(SGU_GROUPS):
            cols = slice(g * LANES, (g + 1) * LANES)
            mixed = jnp.dot(ws_ref[g], gv_scr[rows, cols], preferred_element_type=F32) + bs_ref[g]
            sg_scr[rows, cols] = (u_scr[rows, cols] * mixed).astype(BF16)

    y = jnp.dot(sg_scr[...], wp_ref[...], preferred_element_type=F32)
    gate = _sigmoid(gl_ref[...].astype(F32) + bg_ref[...])
    out_ref[...] = (gate * y).astype(BF16)


def _sgu_branch(z2, bg, lng, lnb, ws, bs, wp):
    def col(c):
        return pl.BlockSpec((SG_TS, D_MODEL), lambda i: (i, c))

    return pl.pallas_call(
        _sgu_kernel,
        grid=(TOKENS // SG_TS,),
        in_specs=[
            col(Z_SU // D_MODEL), col(Z_SV // D_MODEL), col(Z_GC // D_MODEL),
            _resident((1, D_MODEL)),
            _resident((1, D_MODEL)),
            _resident((1, D_MODEL)),
            _resident((SGU_GROUPS, CHUNK, CHUNK)),
            _resident((SGU_GROUPS, CHUNK, LANES)),
            _resident((D_MODEL, D_MODEL)),
        ],
        out_specs=pl.BlockSpec((SG_TS, D_MODEL), lambda i: (i, 0)),
        out_shape=jax.ShapeDtypeStruct((TOKENS, D_MODEL), BF16),
        scratch_shapes=[
            pltpu.VMEM((SG_TS, D_MODEL), F32),
            pltpu.VMEM((SG_TS, D_MODEL), BF16),
            pltpu.VMEM((SG_TS, D_MODEL), BF16),
        ],
        compiler_params=_params(("parallel",)),
        name="sgu_branch",
    )(z2, z2, z2, bg, lng, lnb, ws, bs, wp)


MG_TM = 512


def _merge_kernel(x_ref, ya_ref, o_ref, yc_ref, gl_ref, bg_ref, wa_ref, wout_ref, gpost_ref, out_ref):
    yb = jnp.dot(o_ref[...], wa_ref[...], preferred_element_type=F32)
    gate = _sigmoid(gl_ref[...].astype(F32) + bg_ref[...])
    mix = ya_ref[...].astype(F32) + gate * yb + yc_ref[...].astype(F32)
    m = jnp.dot(mix.astype(BF16), wout_ref[...], preferred_element_type=F32)
    out_ref[...] = x_ref[...] + _rms(m, gpost_ref[...])


def _merge(x2, ya, o, yc, z2, bg, wa, wout, gpost):
    def rows(width=D_MODEL, col=0):
        return pl.BlockSpec((MG_TM, width), lambda i: (i, col))

    return pl.pallas_call(
        _merge_kernel,
        grid=(TOKENS // MG_TM,),
        in_specs=[
            rows(), rows(), rows(), rows(), rows(col=Z_GB // D_MODEL),
            _resident((1, D_MODEL)),
            _resident((D_MODEL, D_MODEL)),
            _resident((D_MODEL, D_MODEL)),
            _resident((1, D_MODEL)),
        ],
        out_specs=rows(),
        out_shape=jax.ShapeDtypeStruct((TOKENS, D_MODEL), F32),
        compiler_params=_params(("parallel",)),
        name="merge",
    )(x2, ya, o, yc, z2, bg, wa, wout, gpost)


FF_TM = 512
FF_TC = 1024


def _ffn_kernel(x_ref, gpre_ref, gpost_ref, wup_ref, wdn_ref, out_ref):
    x = x_ref[...]
    h = _rms(x, gpre_ref[...]).astype(BF16)
    acc = jnp.zeros((FF_TM, D_MODEL), F32)
    for c in range(D_FF // FF_TC):
        cols = slice(c * FF_TC, (c + 1) * FF_TC)
        u = jnp.dot(h, wup_ref[:, cols], preferred_element_type=F32)
        u = jnp.square(jnp.maximum(u, 0.0)).astype(BF16)
        acc = acc + jnp.dot(u, wdn_ref[cols, :], preferred_element_type=F32)
    out_ref[...] = x + _rms(acc, gpost_ref[...])


def _ffn(x2, gpre, gpost, wup, wdn):
    return pl.pallas_call(
        _ffn_kernel,
        grid=(TOKENS // FF_TM,),
        in_specs=[
            pl.BlockSpec((FF_TM, D_MODEL), lambda i: (i, 0)),
            _resident((1, D_MODEL)),
            _resident((1, D_MODEL)),
            _resident((D_MODEL, D_FF)),
            _resident((D_FF, D_MODEL)),
        ],
        out_specs=pl.BlockSpec((FF_TM, D_MODEL), lambda i: (i, 0)),
        out_shape=jax.ShapeDtypeStruct((TOKENS, D_MODEL), F32),
        compiler_params=_params(("parallel",)),
        name="ffn",
    )(x2, gpre, gpost, wup, wdn)


def _row(v):
    return v.reshape(1, -1).astype(F32)


def kernel(x, norm_mix_pre, norm_mix_post, w_in, b_gate, conv_w, conv_b, conv_ln_g, conv_ln_b,
           lam_q1, lam_k1, lam_q2, lam_k2, subln_g, sgu_ln_g, sgu_ln_b, sgu_w, sgu_b,
           w_proj_conv, w_proj_attn, w_proj_sgu, w_out, norm_ffn_pre, norm_ffn_post,
           w_ffn_up, w_ffn_down):
    assert x.shape == (BATCH, SEQ, D_MODEL) and w_in.shape == (DEPTH, D_MODEL, Z_COLS)
    x2 = x.reshape(TOKENS, D_MODEL).astype(F32)

    colscale = jnp.ones((1, Z_COLS), F32).at[:, Z_Q:Z_Q + D_MODEL].set(DH ** -0.5 * LOG2E)
    slopes = jnp.exp2(-8.0 * jnp.arange(1, HEADS + 1, dtype=F32) / HEADS) * LOG2E
    c1 = slopes.astype(BF16).astype(F32)
    c2 = (slopes - c1).astype(BF16).astype(F32)
    c3 = (slopes - c1 - c2).astype(BF16).astype(F32)
    pieces = jnp.stack([c1, c2, c3], axis=1)
    a = jnp.arange(HEAD_W) & (DH - 1)
    kconst = jnp.where(a < 6, -pieces[:, a % 3], 0.0)[:, None, :]
    qconst = jnp.where((a >= 6) & (a < 12), pieces[:, a % 3], 0.0)[:, None, :]
    slopes = jnp.broadcast_to(slopes[:, None, None], (HEADS, 1, AT_TQ))

    for l in range(DEPTH):
        lam_init = 0.8 - 0.6 * math.exp(-0.3 * l)
        cw = conv_w[l].reshape(CONV_K, CV_NC, LANES).transpose(1, 0, 2).astype(F32)
        cb = conv_b[l].reshape(CV_NC, 1, LANES).astype(F32)
        z2, conv_raw = _proj_in(x2, _row(norm_mix_pre[l]), colscale, w_in[l].astype(BF16), cw, cb)
        z3 = z2.reshape(BATCH, SEQ, Z_COLS)

        ya = _conv_branch(conv_raw, z2, _row(b_gate[l, 0:D_MODEL]),
                          _row(conv_ln_g[l]), _row(conv_ln_b[l]), w_proj_conv[l].astype(BF16))

        lam_vecs = jnp.stack([lam_q1[l], lam_q2[l], lam_k1[l], lam_k2[l]]).astype(F32)
        o = _attention(z3, lam_vecs, slopes, kconst, qconst,
                       _row(subln_g[l]) * (1.0 - lam_init), lam_init)

        bs = jnp.broadcast_to(sgu_b[l].astype(F32)[:, :, None], (SGU_GROUPS, CHUNK, LANES))
        yc = _sgu_branch(z2, _row(b_gate[l, 2 * D_MODEL:3 * D_MODEL]), _row(sgu_ln_g[l]),
                         _row(sgu_ln_b[l]), sgu_w[l].astype(BF16), bs, w_proj_sgu[l].astype(BF16))

        x2 = _merge(x2, ya, o.reshape(TOKENS, D_MODEL), yc, z2,
                    _row(b_gate[l, D_MODEL:2 * D_MODEL]), w_proj_attn[l].astype(BF16),
                    w_out[l].astype(BF16), _row(norm_mix_post[l]))

        x2 = _ffn(x2, _row(norm_ffn_pre[l]), _row(norm_ffn_post[l]),
                  w_ffn_up[l].astype(BF16), w_ffn_down[l].astype(BF16))

    return x2.reshape(BATCH, SEQ, D_MODEL).astype(x.dtype)
```

```python
import functools
import math

import jax
import jax.numpy as jnp
from jax import lax
from jax.experimental import pallas as pl
from jax.experimental.pallas import tpu as pltpu

F32 = jnp.float32
BF16 = jnp.bfloat16

D_MODEL = 1024
BATCH = 8
SEQ = 2048
DEPTH = 2
TOKENS = BATCH * SEQ
EPS = 1e-6

CONV_K = 31
CONV_PAD = (CONV_K - 1) // 2
HEADS = 8
DH = 64
HEAD_W = 2 * DH
SGU_GROUPS = 8
CHUNK = 128
D_FF = 4 * D_MODEL

Z_A, Z_B, Z_Q, Z_K, Z_V, Z_SU, Z_SV, Z_GA, Z_GB, Z_GC = (i * D_MODEL for i in range(10))
Z_COLS = 10 * D_MODEL

LANES = 128
SUBLANES = 8
HALO = 16
LOG2E = 1.4426950408889634

VMEM_LIMIT = 56 * 1024 * 1024


def _params(semantics):
    return pltpu.CompilerParams(dimension_semantics=semantics, vmem_limit_bytes=VMEM_LIMIT)


def _resident(shape):
    zeros = (0,) * len(shape)
    return pl.BlockSpec(shape, lambda *_: zeros, pipeline_mode=pl.Buffered(1))


def _layer_weight(shape, layer):
    zeros = (0,) * len(shape)
    return pl.BlockSpec((None,) + shape, lambda *_: (layer,) + zeros, pipeline_mode=pl.Buffered(1))


def _rms(x, g):
    ms = jnp.mean(x * x, axis=-1, keepdims=True)
    return x * lax.rsqrt(ms + EPS) * g


def _sigmoid(x):
    return 0.5 * jnp.tanh(0.5 * x) + 0.5


def _gelu(x):
    return 0.5 * x * (1.0 + lax.erf(x * (2.0 ** -0.5)))


PI_TM = SEQ
PI_TN = 1024
PI_RC = 256
PI_CONV_J0 = 2
CV_R = 128
CV_NC = D_MODEL // LANES
assert Z_A == 0 and Z_B == PI_TN and Z_COLS // PI_TN == PI_CONV_J0 + CV_NC


def _proj_in_kernel(x_ref, g_ref, cs_ref, w_ref, cw_ref, cb_ref, z_ref, cv_ref, h_ref, zab_scr, a_scr):
    j = pl.program_id(1)

    @pl.when(j == 0)
    def _():
        for r in range(PI_TM // PI_RC):
            rows = slice(r * PI_RC, (r + 1) * PI_RC)
            h_ref[rows, :] = _rms(x_ref[rows, :], g_ref[...]).astype(BF16)

    def project(r, keep):
        rows = slice(r * PI_RC, (r + 1) * PI_RC)
        acc = jnp.dot(h_ref[rows, :], w_ref[...], preferred_element_type=F32)
        zb = (acc * cs_ref[...]).astype(BF16)
        z_ref[rows, :] = zb
        if keep:
            for c in range(CV_NC):
                zab_scr[j, c, rows, :] = zb[:, c * LANES:(c + 1) * LANES]

    @pl.when(j < PI_CONV_J0)
    def _():
        for r in range(PI_TM // PI_RC):
            project(r, keep=True)

    @pl.when(j >= PI_CONV_J0)
    def _():
        c = j - PI_CONV_J0
        zero = jnp.zeros((HALO, LANES), F32)
        a_scr[0:HALO, :] = zero
        a_scr[HALO + SEQ:, :] = zero
        for r in range(SEQ // PI_RC):
            rows = slice(r * PI_RC, (r + 1) * PI_RC)
            glu = zab_scr[0, c, rows, :].astype(F32) * _sigmoid(zab_scr[1, c, rows, :].astype(F32))
            a_scr[HALO + r * PI_RC:HALO + (r + 1) * PI_RC, :] = glu
        for r in range(PI_TM // PI_RC):
            project(r, keep=False)
            for rr in range(r * (PI_RC // CV_R), (r + 1) * (PI_RC // CV_R)):
                acc = jnp.zeros((CV_R, LANES), F32)
                for t in range(CONV_K):
                    start = rr * CV_R + HALO - CONV_PAD + t
                    acc = acc + cw_ref[0, t:t + 1, :] * a_scr[start:start + CV_R, :]
                cv_ref[rr * CV_R:(rr + 1) * CV_R, :] = acc + cb_ref[0]


def _proj_in(x2, g, colscale, w, layer, cw, cb):
    def chunk(i, j):
        return jnp.maximum(j - PI_CONV_J0, 0)

    return pl.pallas_call(
        _proj_in_kernel,
        grid=(TOKENS // PI_TM, Z_COLS // PI_TN),
        in_specs=[
            pl.BlockSpec((PI_TM, D_MODEL), lambda i, j: (i, 0)),
            _resident((1, D_MODEL)),
            pl.BlockSpec((1, PI_TN), lambda i, j: (0, j)),
            pl.BlockSpec((None, D_MODEL, PI_TN), lambda i, j: (layer, 0, j)),
            pl.BlockSpec((1, CONV_K, LANES), lambda i, j: (chunk(i, j), 0, 0)),
            pl.BlockSpec((1, 1, LANES), lambda i, j: (chunk(i, j), 0, 0)),
        ],
        out_specs=[
            pl.BlockSpec((PI_TM, PI_TN), lambda i, j: (i, j)),
            pl.BlockSpec((PI_TM, LANES), lambda i, j: (i, chunk(i, j))),
        ],
        out_shape=[
            jax.ShapeDtypeStruct((TOKENS, Z_COLS), BF16),
            jax.ShapeDtypeStruct((TOKENS, D_MODEL), F32),
        ],
        scratch_shapes=[
            pltpu.VMEM((PI_TM, D_MODEL), BF16),
            pltpu.VMEM((PI_CONV_J0, CV_NC, SEQ, LANES), BF16),
            pltpu.VMEM((SEQ + 2 * HALO, LANES), F32),
        ],
        compiler_params=_params(("parallel", "arbitrary")),
        name="proj_in",
    )(x2, g, colscale, w, cw, cb)


CV_TS = 512
CV_RC = 128


def _conv_kernel(cv_ref, gl_ref, bg_ref, lng_ref, lnb_ref, wp_ref, out_ref, act_scr):
    for r in range(CV_TS // CV_RC):
        rows = slice(r * CV_RC, (r + 1) * CV_RC)
        c = cv_ref[rows, :]
        mu = jnp.mean(c, axis=-1, keepdims=True)
        d = c - mu
        var = jnp.mean(d * d, axis=-1, keepdims=True)
        y = d * lax.rsqrt(var + EPS) * lng_ref[...] + lnb_ref[...]
        act_scr[rows, :] = (y * _sigmoid(y)).astype(BF16)

    y = jnp.dot(act_scr[...], wp_ref[...], preferred_element_type=F32)
    gate = _sigmoid(gl_ref[...].astype(F32) + bg_ref[...])
    out_ref[...] = (gate * y).astype(BF16)


def _conv_branch(conv_raw, z2, bg, lng, lnb, wp, layer):
    return pl.pallas_call(
        _conv_kernel,
        grid=(TOKENS // CV_TS,),
        in_specs=[
            pl.BlockSpec((CV_TS, D_MODEL), lambda i: (i, 0)),
            pl.BlockSpec((CV_TS, D_MODEL), lambda i: (i, Z_GA // D_MODEL)),
            _resident((1, D_MODEL)),
            _resident((1, D_MODEL)),
            _resident((1, D_MODEL)),
            _layer_weight((D_MODEL, D_MODEL), layer),
        ],
        out_specs=pl.BlockSpec((CV_TS, D_MODEL), lambda i: (i, 0)),
        out_shape=jax.ShapeDtypeStruct((TOKENS, D_MODEL), BF16),
        scratch_shapes=[pltpu.VMEM((CV_TS, D_MODEL), BF16)],
        compiler_params=_params(("parallel",)),
        name="conv_branch",
    )(conv_raw, z2, bg, lng, lnb, wp)


AT_TQ = 256
AT_NB = 4
AT_RB = 64
AT_KT = 256
AT_NKB = SEQ // AT_KT
AT_NKB_LOG2 = AT_NKB.bit_length() - 1
AT_BLK_SCR = 5
assert AT_NKB == 1 << AT_NKB_LOG2 and AT_KT == AT_TQ


def _score_softmax_pass(score_jobs, softmax_jobs, corr_ref):
    accs = [None] * len(score_jobs)
    sums = [jnp.zeros((SUBLANES, AT_TQ), F32)] * len(softmax_jobs)
    tiled = (AT_RB // SUBLANES, SUBLANES, AT_TQ)
    for t in range(AT_NKB):
        for i, (qk, s_ref) in enumerate(score_jobs):
            s = qk(t)
            for r in range(AT_KT // AT_RB):
                tile = s[r * AT_RB:(r + 1) * AT_RB, :]
                if t == 0:
                    tile = tile + corr_ref[r * AT_RB:(r + 1) * AT_RB, :]
                s_ref[t * AT_KT + r * AT_RB:t * AT_KT + (r + 1) * AT_RB, :] = tile
                part = jnp.max(tile.reshape(tiled), axis=0)
                accs[i] = part if accs[i] is None else jnp.maximum(accs[i], part)
        for i, (s_ref, m, p_ref) in enumerate(softmax_jobs):
            for r in range(AT_KT // AT_RB):
                rows = slice(t * AT_KT + r * AT_RB, t * AT_KT + (r + 1) * AT_RB)
                p = jnp.exp2(s_ref[rows, :].reshape(tiled) - m[None])
                sums[i] = sums[i] + jnp.sum(p, axis=0)
                p_ref[rows, :] = p.reshape(AT_RB, AT_TQ).astype(BF16)
    maxes = [jnp.broadcast_to(jnp.max(a, axis=0, keepdims=True), (SUBLANES, AT_TQ)) for a in accs]
    return maxes, [jnp.sum(x, axis=0, keepdims=True) for x in sums]


def _attn_kernel(lam_init, lam_ref, slope_ref, kc_ref, qc_ref, q_ref, k_ref, v_ref, g_ref, o_ref,
                 kaug_scr, qbase_scr, corr_scr, klr0_scr, klr1_scr, vt_scr, *per_block):
    klr_scr = (klr0_scr, klr1_scr)
    blk_scr = [per_block[AT_BLK_SCR * n:AT_BLK_SCR * (n + 1)] for n in range(AT_NB)]
    b = pl.program_id(1)
    step = pl.program_id(2)
    lane1 = lax.broadcasted_iota(jnp.int32, (1, HEAD_W), 1) & (DH - 1)

    @pl.when((b == 0) & (step == 0))
    def _():
        is_jl = ((lane1 >= 6) & (lane1 < 9)).astype(F32)
        is_jh = ((lane1 >= 9) & (lane1 < 12)).astype(F32)
        is_il = (lane1 < 3).astype(F32)
        row = lax.broadcasted_iota(jnp.int32, (SEQ, HEAD_W), 0)
        jl = (row & (AT_TQ - 1)).astype(F32)
        jh = (row - (row & (AT_TQ - 1))).astype(F32)
        kaug_scr[...] = kc_ref[0] + is_jl * jl + is_jh * jh
        il = lax.broadcasted_iota(jnp.int32, (AT_TQ, HEAD_W), 0).astype(F32)
        qbase_scr[...] = qc_ref[0] + is_il * il
        kk = lax.broadcasted_iota(jnp.int32, (AT_TQ, AT_TQ), 0)
        qq = lax.broadcasted_iota(jnp.int32, (AT_TQ, AT_TQ), 1)
        corr_scr[...] = -2.0 * slope_ref[0] * jnp.maximum(kk - qq, 0).astype(F32)

    @pl.when(step == 0)
    def _():
        half_k = lax.broadcasted_iota(jnp.int32, (SEQ, HEAD_W), 1) >> 6
        kf = k_ref[0].astype(F32)
        kaug = kaug_scr[...]
        for mp in range(2):
            klr_scr[mp][0] = jnp.where(half_k == mp, kf, kaug).astype(BF16)
            klr_scr[mp][1] = jnp.where(half_k == mp, kf, -kaug).astype(BF16)
        for j in range(AT_NKB):
            vt_scr[j] = v_ref[0, j * AT_KT:(j + 1) * AT_KT, :].astype(F32).T.astype(BF16)

    lam_v = lam_ref[...]
    dots = jnp.sum(lam_v[0:2, :] * lam_v[2:4, :], axis=-1, keepdims=True)
    e = jnp.exp(dots)
    lam = e[0:1, :] - e[1:2, :] + lam_init

    is_q0 = ((lane1 >= 3) & (lane1 < 6)).astype(F32)
    half_q = lax.broadcasted_iota(jnp.int32, (AT_TQ, HEAD_W), 1) >> 6

    def qk_stage(n):
        vtr_scr, s_scr = blk_scr[n][0], blk_scr[n][1:3]
        qi = step * AT_NB + n
        kblk = [(qi + t) & (AT_NKB - 1) for t in range(AT_NKB)]
        for t in range(AT_NKB):
            vtr_scr[:, t * AT_KT:(t + 1) * AT_KT] = vt_scr[kblk[t]]
        qf = q_ref[0, n * AT_TQ:(n + 1) * AT_TQ, :].astype(F32)
        qaug = qbase_scr[...] + is_q0 * (qi * AT_TQ).astype(F32)
        jobs = []
        for mp in range(2):
            qat = jnp.where(half_q == mp, qf, qaug).T.astype(BF16)

            def qk(t, mp=mp, qat=qat):
                side = 0 if t == 0 else 1 - ((qi + t) >> AT_NKB_LOG2)
                src = pl.ds(pl.multiple_of(kblk[t] * AT_KT, AT_KT), AT_KT)
                return jnp.dot(klr_scr[mp][side, src, :], qat, preferred_element_type=F32)

            jobs.append((qk, s_scr[mp]))
        return jobs

    def softmax_jobs(n, mx):
        s_scr, p_scr = blk_scr[n][1:3], blk_scr[n][3:5]
        return [(s_scr[mp], mx[mp], p_scr[mp]) for mp in range(2)]

    def pv_stage(n, denom):
        vtr_scr, p_scr = blk_scr[n][0], blk_scr[n][3:5]
        outs = [jnp.dot(vtr_scr[...], p_scr[mp][...], preferred_element_type=F32) / denom[mp]
                for mp in range(2)]
        ot = outs[0] - lam * outs[1]
        ms = jnp.mean(ot * ot, axis=0, keepdims=True)
        ot = ot * lax.rsqrt(ms + EPS)
        o_ref[0, n * AT_TQ:(n + 1) * AT_TQ, :] = (ot.T * g_ref[...]).astype(BF16)

    mx, _ = _score_softmax_pass(qk_stage(0), [], corr_scr)
    for n in range(AT_NB):
        nxt = qk_stage(n + 1) if n + 1 < AT_NB else []
        mx, denom = _score_softmax_pass(nxt, softmax_jobs(n, mx), corr_scr)
        pv_stage(n, denom)


def _attention(z3, lam_vecs, slopes, kconst, qconst, gain, lam_init):
    qc, kc, vc = Z_Q // HEAD_W, Z_K // HEAD_W, Z_V // HEAD_W
    return pl.pallas_call(
        functools.partial(_attn_kernel, lam_init),
        grid=(HEADS, BATCH, SEQ // (AT_TQ * AT_NB)),
        in_specs=[
            _resident((4, DH)),
            pl.BlockSpec((1, 1, AT_TQ), lambda h, b, i: (h, 0, 0)),
            pl.BlockSpec((1, 1, HEAD_W), lambda h, b, i: (h, 0, 0)),
            pl.BlockSpec((1, 1, HEAD_W), lambda h, b, i: (h, 0, 0)),
            pl.BlockSpec((1, AT_TQ * AT_NB, HEAD_W), lambda h, b, i: (b, i, qc + h)),
            pl.BlockSpec((1, SEQ, HEAD_W), lambda h, b, i: (b, 0, kc + h)),
            pl.BlockSpec((1, SEQ, HEAD_W), lambda h, b, i: (b, 0, vc + h)),
            _resident((1, HEAD_W)),
        ],
        out_specs=pl.BlockSpec((1, AT_TQ * AT_NB, HEAD_W), lambda h, b, i: (b, i, h)),
        out_shape=jax.ShapeDtypeStruct((BATCH, SEQ, HEADS * HEAD_W), BF16),
        scratch_shapes=[
            pltpu.VMEM((SEQ, HEAD_W), F32),
            pltpu.VMEM((AT_TQ, HEAD_W), F32),
            pltpu.VMEM((AT_TQ, AT_TQ), F32),
            pltpu.VMEM((2, SEQ, HEAD_W), BF16),
            pltpu.VMEM((2, SEQ, HEAD_W), BF16),
            pltpu.VMEM((AT_NKB, HEAD_W, AT_KT), BF16),
        ] + AT_NB * [
            pltpu.VMEM((HEAD_W, SEQ), BF16),
            pltpu.VMEM((SEQ, AT_TQ), F32),
            pltpu.VMEM((SEQ, AT_TQ), F32),
            pltpu.VMEM((SEQ, AT_TQ), BF16),
            pltpu.VMEM((SEQ, AT_TQ), BF16),
        ],
        compiler_params=_params(("arbitrary", "arbitrary", "arbitrary")),
        name="diff_attention",
    )(lam_vecs, slopes, kconst, qconst, z3, z3, z3, gain)


SG_TS = 256


def _sgu_kernel(su_ref, sv_ref, gl_ref, bg_ref, lng_ref, lnb_ref, ws_ref, bs_ref, wp_ref,
                out_ref, u_scr, gv_scr, sg_scr):
    v = _gelu(sv_ref[...].astype(F32))
    mu = jnp.mean(v, axis=-1, keepdims=True)
    vc = v - mu
    var = jnp.mean(vc * vc, axis=-1, keepdims=True)
    gv_scr[...] = (vc * lax.rsqrt(var + EPS) * lng_ref[...] + lnb_ref[...]).astype(BF16)
    u_scr[...] = _gelu(su_ref[...].astype(F32))

    for n in range(SG_TS // CHUNK):
        rows = slice(n * CHUNK, (n + 1) * CHUNK)
        for g in range(SGU_GROUPS):
            cols = slice(g * LANES, (g + 1) * LANES)
            mixed = jnp.dot(ws_ref[g], gv_scr[rows, cols], preferred_element_type=F32) + bs_ref[g]
            sg_scr[rows, cols] = (u_scr[rows, cols] * mixed).astype(BF16)

    y = jnp.dot(sg_scr[...], wp_ref[...], preferred_element_type=F32)
    gate = _sigmoid(gl_ref[...].astype(F32) + bg_ref[...])
    out_ref[...] = (gate * y).astype(BF16)


def _sgu_branch(z2, bg, lng, lnb, ws, bs, wp, layer):
    def col(c):
        return pl.BlockSpec((SG_TS, D_MODEL), lambda i: (i, c))

    return pl.pallas_call(
        _sgu_kernel,
        grid=(TOKENS // SG_TS,),
        in_specs=[
            col(Z_SU // D_MODEL), col(Z_SV // D_MODEL), col(Z_GC // D_MODEL),
            _resident((1, D_MODEL)),
            _resident((1, D_MODEL)),
            _resident((1, D_MODEL)),
            _layer_weight((SGU_GROUPS, CHUNK, CHUNK), layer),
            _resident((SGU_GROUPS, CHUNK, LANES)),
            _layer_weight((D_MODEL, D_MODEL), layer),
        ],
        out_specs=pl.BlockSpec((SG_TS, D_MODEL), lambda i: (i, 0)),
        out_shape=jax.ShapeDtypeStruct((TOKENS, D_MODEL), BF16),
        scratch_shapes=[
            pltpu.VMEM((SG_TS, D_MODEL), F32),
            pltpu.VMEM((SG_TS, D_MODEL), BF16),
            pltpu.VMEM((SG_TS, D_MODEL), BF16),
        ],
        compiler_params=_params(("parallel",)),
        name="sgu_branch",
    )(z2, z2, z2, bg, lng, lnb, ws, bs, wp)


MG_TM = 512


def _merge_kernel(x_ref, ya_ref, o_ref, yc_ref, gl_ref, bg_ref, wa_ref, wout_ref, gpost_ref, out_ref):
    yb = jnp.dot(o_ref[...], wa_ref[...], preferred_element_type=F32)
    gate = _sigmoid(gl_ref[...].astype(F32) + bg_ref[...])
    mix = ya_ref[...].astype(F32) + gate * yb + yc_ref[...].astype(F32)
    m = jnp.dot(mix.astype(BF16), wout_ref[...], preferred_element_type=F32)
    out_ref[...] = x_ref[...] + _rms(m, gpost_ref[...])


def _merge(x2, ya, o, yc, z2, bg, wa, wout, gpost, layer):
    def rows(width=D_MODEL, col=0):
        return pl.BlockSpec((MG_TM, width), lambda i: (i, col))

    return pl.pallas_call(
        _merge_kernel,
        grid=(TOKENS // MG_TM,),
        in_specs=[
            rows(), rows(), rows(), rows(), rows(col=Z_GB // D_MODEL),
            _resident((1, D_MODEL)),
            _layer_weight((D_MODEL, D_MODEL), layer),
            _layer_weight((D_MODEL, D_MODEL), layer),
            _resident((1, D_MODEL)),
        ],
        out_specs=rows(),
        out_shape=jax.ShapeDtypeStruct((TOKENS, D_MODEL), F32),
        compiler_params=_params(("parallel",)),
        name="merge",
    )(x2, ya, o, yc, z2, bg, wa, wout, gpost)


FF_TM = 512
FF_TC = 1024


def _ffn_kernel(x_ref, gpre_ref, gpost_ref, wup_ref, wdn_ref, out_ref):
    x = x_ref[...]
    h = _rms(x, gpre_ref[...]).astype(BF16)
    acc = jnp.zeros((FF_TM, D_MODEL), F32)
    for c in range(D_FF // FF_TC):
        cols = slice(c * FF_TC, (c + 1) * FF_TC)
        u = jnp.dot(h, wup_ref[:, cols], preferred_element_type=F32)
        u = jnp.square(jnp.maximum(u, 0.0)).astype(BF16)
        acc = acc + jnp.dot(u, wdn_ref[cols, :], preferred_element_type=F32)
    out_ref[...] = x + _rms(acc, gpost_ref[...])


def _ffn(x2, gpre, gpost, wup, wdn, layer):
    return pl.pallas_call(
        _ffn_kernel,
        grid=(TOKENS // FF_TM,),
        in_specs=[
            pl.BlockSpec((FF_TM, D_MODEL), lambda i: (i, 0)),
            _resident((1, D_MODEL)),
            _resident((1, D_MODEL)),
            _layer_weight((D_MODEL, D_FF), layer),
            _layer_weight((D_FF, D_MODEL), layer),
        ],
        out_specs=pl.BlockSpec((FF_TM, D_MODEL), lambda i: (i, 0)),
        out_shape=jax.ShapeDtypeStruct((TOKENS, D_MODEL), F32),
        compiler_params=_params(("parallel",)),
        name="ffn",
    )(x2, gpre, gpost, wup, wdn)


def _row(v):
    return v.reshape(1, -1).astype(F32)


def kernel(x, norm_mix_pre, norm_mix_post, w_in, b_gate, conv_w, conv_b, conv_ln_g, conv_ln_b,
           lam_q1, lam_k1, lam_q2, lam_k2, subln_g, sgu_ln_g, sgu_ln_b, sgu_w, sgu_b,
           w_proj_conv, w_proj_attn, w_proj_sgu, w_out, norm_ffn_pre, norm_ffn_post,
           w_ffn_up, w_ffn_down):
    assert x.shape == (BATCH, SEQ, D_MODEL) and w_in.shape == (DEPTH, D_MODEL, Z_COLS)
    x2 = x.reshape(TOKENS, D_MODEL).astype(F32)

    colscale = jnp.ones((1, Z_COLS), F32).at[:, Z_Q:Z_Q + D_MODEL].set(DH ** -0.5 * LOG2E)
    slopes = jnp.exp2(-8.0 * jnp.arange(1, HEADS + 1, dtype=F32) / HEADS) * LOG2E
    c1 = slopes.astype(BF16).astype(F32)
    c2 = (slopes - c1).astype(BF16).astype(F32)
    c3 = (slopes - c1 - c2).astype(BF16).astype(F32)
    pieces = jnp.stack([c1, c2, c3], axis=1)
    a = jnp.arange(HEAD_W) & (DH - 1)
    kconst = jnp.where(a < 6, -pieces[:, a % 3], 0.0)[:, None, :]
    qconst = jnp.where((a >= 6) & (a < 12), pieces[:, a % 3], 0.0)[:, None, :]
    slopes = jnp.broadcast_to(slopes[:, None, None], (HEADS, 1, AT_TQ))

    w_in_b, w_conv_b, w_attn_b, w_sgu_b, w_out_b, sgu_w_b, w_up_b, w_dn_b = (
        w.astype(BF16) for w in (w_in, w_proj_conv, w_proj_attn, w_proj_sgu, w_out, sgu_w,
                                 w_ffn_up, w_ffn_down))

    for l in range(DEPTH):
        lam_init = 0.8 - 0.6 * math.exp(-0.3 * l)
        cw = conv_w[l].reshape(CONV_K, CV_NC, LANES).transpose(1, 0, 2).astype(F32)
        cb = conv_b[l].reshape(CV_NC, 1, LANES).astype(F32)
        z2, conv_raw = _proj_in(x2, _row(norm_mix_pre[l]), colscale, w_in_b, l, cw, cb)
        z3 = z2.reshape(BATCH, SEQ, Z_COLS)

        ya = _conv_branch(conv_raw, z2, _row(b_gate[l, 0:D_MODEL]),
                          _row(conv_ln_g[l]), _row(conv_ln_b[l]), w_conv_b, l)

        lam_vecs = jnp.stack([lam_q1[l], lam_q2[l], lam_k1[l], lam_k2[l]]).astype(F32)
        o = _attention(z3, lam_vecs, slopes, kconst, qconst,
                       _row(subln_g[l]) * (1.0 - lam_init), lam_init)

        bs = jnp.broadcast_to(sgu_b[l].astype(F32)[:, :, None], (SGU_GROUPS, CHUNK, LANES))
        yc = _sgu_branch(z2, _row(b_gate[l, 2 * D_MODEL:3 * D_MODEL]), _row(sgu_ln_g[l]),
                         _row(sgu_ln_b[l]), sgu_w_b, bs, w_sgu_b, l)

        x2 = _merge(x2, ya, o.reshape(TOKENS, D_MODEL), yc, z2,
                    _row(b_gate[l, D_MODEL:2 * D_MODEL]), w_attn_b, w_out_b,
                    _row(norm_mix_post[l]), l)

        x2 = _ffn(x2, _row(norm_ffn_pre[l]), _row(norm_ffn_post[l]), w_up_b, w_dn_b, l)

    return x2.reshape(BATCH, SEQ, D_MODEL).astype(x.dtype)
```

```python
import functools
import math

import jax
import jax.numpy as jnp
from jax import lax
from jax.experimental import pallas as pl
from jax.experimental.pallas import tpu as pltpu

F32 = jnp.float32
BF16 = jnp.bfloat16

D_MODEL = 1024
BATCH = 8
SEQ = 2048
DEPTH = 2
TOKENS = BATCH * SEQ
EPS = 1e-6

CONV_K = 31
CONV_PAD = (CONV_K - 1) // 2
HEADS = 8
DH = 64
HEAD_W = 2 * DH
SGU_GROUPS = 8
CHUNK = 128
D_FF = 4 * D_MODEL

Z_A, Z_B, Z_Q, Z_K, Z_V, Z_SU, Z_SV, Z_GA, Z_GB, Z_GC = (i * D_MODEL for i in range(10))
Z_COLS = 10 * D_MODEL

LANES = 128
SUBLANES = 8
HALO = 16
LOG2E = 1.4426950408889634

VMEM_LIMIT = 56 * 1024 * 1024


def _params(semantics):
    return pltpu.CompilerParams(dimension_semantics=semantics, vmem_limit_bytes=VMEM_LIMIT)


def _resident(shape):
    zeros = (0,) * len(shape)
    return pl.BlockSpec(shape, lambda *_: zeros, pipeline_mode=pl.Buffered(1))


def _layer_weight(shape, layer):
    zeros = (0,) * len(shape)
    return pl.BlockSpec((None,) + shape, lambda *_: (layer,) + zeros, pipeline_mode=pl.Buffered(1))


def _rms(x, g):
    ms = jnp.mean(x * x, axis=-1, keepdims=True)
    return x * lax.rsqrt(ms + EPS) * g


def _sigmoid(x):
    return 0.5 * jnp.tanh(0.5 * x) + 0.5


def _gelu(x):
    return 0.5 * x * (1.0 + lax.erf(x * (2.0 ** -0.5)))


PI_TM = SEQ
PI_TN = 1024
PI_RC = 256
PI_CONV_J0 = 2
PI_QKV_J0 = Z_Q // PI_TN
assert PI_QKV_J0 >= PI_CONV_J0 and Z_K == Z_Q + PI_TN and Z_V == Z_K + PI_TN
CV_R = 128
CV_NC = D_MODEL // LANES
assert Z_A == 0 and Z_B == PI_TN and Z_COLS // PI_TN == PI_CONV_J0 + CV_NC


def _proj_in_kernel(x_ref, g_ref, cs_ref, w_ref, cw_ref, cb_ref, z_ref, cv_ref, qkv_ref,
                    h_ref, zab_scr, a_scr):
    j = pl.program_id(1)

    @pl.when(j == 0)
    def _():
        for r in range(PI_TM // PI_RC):
            rows = slice(r * PI_RC, (r + 1) * PI_RC)
            h_ref[rows, :] = _rms(x_ref[rows, :], g_ref[...]).astype(BF16)

    def project(r, keep=False, heads=False):
        rows = slice(r * PI_RC, (r + 1) * PI_RC)
        acc = jnp.dot(h_ref[rows, :], w_ref[...], preferred_element_type=F32)
        zb = (acc * cs_ref[...]).astype(BF16)
        z_ref[rows, :] = zb
        if keep:
            for c in range(CV_NC):
                zab_scr[j, c, rows, :] = zb[:, c * LANES:(c + 1) * LANES]
        if heads:
            for hd in range(HEADS):
                qkv_ref[hd, rows, :] = zb[:, hd * HEAD_W:(hd + 1) * HEAD_W]

    def project_and_convolve(heads):
        c = j - PI_CONV_J0
        zero = jnp.zeros((HALO, LANES), F32)
        a_scr[0:HALO, :] = zero
        a_scr[HALO + SEQ:, :] = zero
        for r in range(SEQ // PI_RC):
            rows = slice(r * PI_RC, (r + 1) * PI_RC)
            glu = zab_scr[0, c, rows, :].astype(F32) * _sigmoid(zab_scr[1, c, rows, :].astype(F32))
            a_scr[HALO + r * PI_RC:HALO + (r + 1) * PI_RC, :] = glu
        for r in range(PI_TM // PI_RC):
            project(r, heads=heads)
            for rr in range(r * (PI_RC // CV_R), (r + 1) * (PI_RC // CV_R)):
                acc = jnp.zeros((CV_R, LANES), F32)
                for t in range(CONV_K):
                    start = rr * CV_R + HALO - CONV_PAD + t
                    acc = acc + cw_ref[0, t:t + 1, :] * a_scr[start:start + CV_R, :]
                cv_ref[rr * CV_R:(rr + 1) * CV_R, :] = acc + cb_ref[0]

    @pl.when(j < PI_CONV_J0)
    def _():
        for r in range(PI_TM // PI_RC):
            project(r, keep=True)

    @pl.when((j >= PI_QKV_J0) & (j < PI_QKV_J0 + 3))
    def _():
        project_and_convolve(heads=True)

    @pl.when(j >= PI_QKV_J0 + 3)
    def _():
        project_and_convolve(heads=False)


def _proj_in(x2, g, colscale, w, layer, cw, cb):
    def chunk(i, j):
        return jnp.maximum(j - PI_CONV_J0, 0)

    def qkv(i, j):
        return jnp.clip(j - PI_QKV_J0, 0, 2)

    return pl.pallas_call(
        _proj_in_kernel,
        grid=(TOKENS // PI_TM, Z_COLS // PI_TN),
        in_specs=[
            pl.BlockSpec((PI_TM, D_MODEL), lambda i, j: (i, 0), pipeline_mode=pl.Buffered(1)),
            _resident((1, D_MODEL)),
            pl.BlockSpec((1, PI_TN), lambda i, j: (0, j)),
            pl.BlockSpec((None, D_MODEL, PI_TN), lambda i, j: (layer, 0, j)),
            pl.BlockSpec((1, CONV_K, LANES), lambda i, j: (chunk(i, j), 0, 0)),
            pl.BlockSpec((1, 1, LANES), lambda i, j: (chunk(i, j), 0, 0)),
        ],
        out_specs=[
            pl.BlockSpec((PI_TM, PI_TN), lambda i, j: (i, j)),
            pl.BlockSpec((PI_TM, LANES), lambda i, j: (i, chunk(i, j))),
            pl.BlockSpec((None, HEADS, PI_TM, HEAD_W), lambda i, j: (qkv(i, j), 0, i, 0)),
        ],
        out_shape=[
            jax.ShapeDtypeStruct((TOKENS, Z_COLS), BF16),
            jax.ShapeDtypeStruct((TOKENS, D_MODEL), F32),
            jax.ShapeDtypeStruct((3, HEADS, TOKENS, HEAD_W), BF16),
        ],
        scratch_shapes=[
            pltpu.VMEM((PI_TM, D_MODEL), BF16),
            pltpu.VMEM((PI_CONV_J0, CV_NC, SEQ, LANES), BF16),
            pltpu.VMEM((SEQ + 2 * HALO, LANES), F32),
        ],
        compiler_params=_params(("parallel", "arbitrary")),
        name="proj_in",
    )(x2, g, colscale, w, cw, cb)


CV_TS = 1024
CV_RC = 128
CV_PC = 256


def _conv_kernel(cv_ref, gl_ref, bg_ref, lng_ref, lnb_ref, wp_ref, out_ref, act_scr):
    for r in range(CV_TS // CV_RC):
        rows = slice(r * CV_RC, (r + 1) * CV_RC)
        c = cv_ref[rows, :]
        mu = jnp.mean(c, axis=-1, keepdims=True)
        d = c - mu
        var = jnp.mean(d * d, axis=-1, keepdims=True)
        y = d * lax.rsqrt(var + EPS) * lng_ref[...] + lnb_ref[...]
        act_scr[rows, :] = (y * _sigmoid(y)).astype(BF16)

    for r in range(CV_TS // CV_PC):
        rows = slice(r * CV_PC, (r + 1) * CV_PC)
        y = jnp.dot(act_scr[rows, :], wp_ref[...], preferred_element_type=F32)
        gate = _sigmoid(gl_ref[rows, :].astype(F32) + bg_ref[...])
        out_ref[rows, :] = (gate * y).astype(BF16)


def _conv_branch(conv_raw, z2, bg, lng, lnb, wp, layer):
    return pl.pallas_call(
        _conv_kernel,
        grid=(TOKENS // CV_TS,),
        in_specs=[
            pl.BlockSpec((CV_TS, D_MODEL), lambda i: (i, 0)),
            pl.BlockSpec((CV_TS, D_MODEL), lambda i: (i, Z_GA // D_MODEL)),
            _resident((1, D_MODEL)),
            _resident((1, D_MODEL)),
            _resident((1, D_MODEL)),
            _layer_weight((D_MODEL, D_MODEL), layer),
        ],
        out_specs=pl.BlockSpec((CV_TS, D_MODEL), lambda i: (i, 0)),
        out_shape=jax.ShapeDtypeStruct((TOKENS, D_MODEL), BF16),
        scratch_shapes=[pltpu.VMEM((CV_TS, D_MODEL), BF16)],
        compiler_params=_params(("parallel",)),
        name="conv_branch",
    )(conv_raw, z2, bg, lng, lnb, wp)


AT_TQ = 256
AT_NB = 4
AT_RB = 64
AT_KT = 256
AT_NKB = SEQ // AT_KT
AT_NKB_LOG2 = AT_NKB.bit_length() - 1
AT_BLK_SCR = 5
assert AT_NKB == 1 << AT_NKB_LOG2 and AT_KT == AT_TQ


def _score_softmax_pass(score_jobs, softmax_jobs, corr_ref):
    accs = [None] * len(score_jobs)
    sums = [jnp.zeros((SUBLANES, AT_TQ), F32)] * len(softmax_jobs)
    tiled = (AT_RB // SUBLANES, SUBLANES, AT_TQ)
    for t in range(AT_NKB):
        for i, (qk, s_ref) in enumerate(score_jobs):
            s = qk(t)
            for r in range(AT_KT // AT_RB):
                tile = s[r * AT_RB:(r + 1) * AT_RB, :]
                if t == 0:
                    tile = tile + corr_ref[r * AT_RB:(r + 1) * AT_RB, :]
                s_ref[t * AT_KT + r * AT_RB:t * AT_KT + (r + 1) * AT_RB, :] = tile
                part = jnp.max(tile.reshape(tiled), axis=0)
                accs[i] = part if accs[i] is None else jnp.maximum(accs[i], part)
        for i, (s_ref, m, p_ref) in enumerate(softmax_jobs):
            for r in range(AT_KT // AT_RB):
                rows = slice(t * AT_KT + r * AT_RB, t * AT_KT + (r + 1) * AT_RB)
                p = jnp.exp2(s_ref[rows, :].reshape(tiled) - m[None])
                sums[i] = sums[i] + jnp.sum(p, axis=0)
                p_ref[rows, :] = p.reshape(AT_RB, AT_TQ).astype(BF16)
    maxes = [jnp.broadcast_to(jnp.max(a, axis=0, keepdims=True), (SUBLANES, AT_TQ)) for a in accs]
    return maxes, [jnp.sum(x, axis=0, keepdims=True) for x in sums]


def _attn_kernel(lam_init, lam_ref, slope_ref, kc_ref, qc_ref, q_ref, k_ref, v_ref, g_ref, o_ref,
                 kaug_scr, qbase_scr, corr_scr, klr0_scr, klr1_scr, vt_scr, *per_block):
    klr_scr = (klr0_scr, klr1_scr)
    blk_scr = [per_block[AT_BLK_SCR * n:AT_BLK_SCR * (n + 1)] for n in range(AT_NB)]
    b = pl.program_id(1)
    step = pl.program_id(2)
    lane1 = lax.broadcasted_iota(jnp.int32, (1, HEAD_W), 1) & (DH - 1)

    @pl.when((b == 0) & (step == 0))
    def _():
        is_jl = ((lane1 >= 6) & (lane1 < 9)).astype(F32)
        is_jh = ((lane1 >= 9) & (lane1 < 12)).astype(F32)
        is_il = (lane1 < 3).astype(F32)
        row = lax.broadcasted_iota(jnp.int32, (SEQ, HEAD_W), 0)
        jl = (row & (AT_TQ - 1)).astype(F32)
        jh = (row - (row & (AT_TQ - 1))).astype(F32)
        kaug_scr[...] = kc_ref[0] + is_jl * jl + is_jh * jh
        il = lax.broadcasted_iota(jnp.int32, (AT_TQ, HEAD_W), 0).astype(F32)
        qbase_scr[...] = qc_ref[0] + is_il * il
        kk = lax.broadcasted_iota(jnp.int32, (AT_TQ, AT_TQ), 0)
        qq = lax.broadcasted_iota(jnp.int32, (AT_TQ, AT_TQ), 1)
        corr_scr[...] = -2.0 * slope_ref[0] * jnp.maximum(kk - qq, 0).astype(F32)

    @pl.when(step == 0)
    def _():
        half_k = lax.broadcasted_iota(jnp.int32, (SEQ, HEAD_W), 1) >> 6
        kf = k_ref[...].astype(F32)
        kaug = kaug_scr[...]
        for mp in range(2):
            klr_scr[mp][0] = jnp.where(half_k == mp, kf, kaug).astype(BF16)
            klr_scr[mp][1] = jnp.where(half_k == mp, kf, -kaug).astype(BF16)
        for j in range(AT_NKB):
            vt_scr[j] = v_ref[j * AT_KT:(j + 1) * AT_KT, :].astype(F32).T.astype(BF16)

    lam_v = lam_ref[...]
    dots = jnp.sum(lam_v[0:2, :] * lam_v[2:4, :], axis=-1, keepdims=True)
    e = jnp.exp(dots)
    lam = e[0:1, :] - e[1:2, :] + lam_init

    is_q0 = ((lane1 >= 3) & (lane1 < 6)).astype(F32)
    half_q = lax.broadcasted_iota(jnp.int32, (AT_TQ, HEAD_W), 1) >> 6

    def qk_stage(n):
        vtr_scr, s_scr = blk_scr[n][0], blk_scr[n][1:3]
        qi = step * AT_NB + n
        kblk = [(qi + t) & (AT_NKB - 1) for t in range(AT_NKB)]
        for t in range(AT_NKB):
            vtr_scr[:, t * AT_KT:(t + 1) * AT_KT] = vt_scr[kblk[t]]
        qf = q_ref[n * AT_TQ:(n + 1) * AT_TQ, :].astype(F32)
        qaug = qbase_scr[...] + is_q0 * (qi * AT_TQ).astype(F32)
        jobs = []
        for mp in range(2):
            qat = jnp.where(half_q == mp, qf, qaug).T.astype(BF16)

            def qk(t, mp=mp, qat=qat):
                side = 0 if t == 0 else 1 - ((qi + t) >> AT_NKB_LOG2)
                src = pl.ds(pl.multiple_of(kblk[t] * AT_KT, AT_KT), AT_KT)
                return jnp.dot(klr_scr[mp][side, src, :], qat, preferred_element_type=F32)

            jobs.append((qk, s_scr[mp]))
        return jobs

    def softmax_jobs(n, mx):
        s_scr, p_scr = blk_scr[n][1:3], blk_scr[n][3:5]
        return [(s_scr[mp], mx[mp], p_scr[mp]) for mp in range(2)]

    def pv_stage(n, denom):
        vtr_scr, p_scr = blk_scr[n][0], blk_scr[n][3:5]
        outs = [jnp.dot(vtr_scr[...], p_scr[mp][...], preferred_element_type=F32) / denom[mp]
                for mp in range(2)]
        ot = outs[0] - lam * outs[1]
        ms = jnp.mean(ot * ot, axis=0, keepdims=True)
        ot = ot * lax.rsqrt(ms + EPS)
        o_ref[0, n * AT_TQ:(n + 1) * AT_TQ, :] = (ot.T * g_ref[...]).astype(BF16)

    mx, _ = _score_softmax_pass(qk_stage(0), [], corr_scr)
    for n in range(AT_NB):
        nxt = qk_stage(n + 1) if n + 1 < AT_NB else []
        mx, denom = _score_softmax_pass(nxt, softmax_jobs(n, mx), corr_scr)
        pv_stage(n, denom)


def _attention(qkv, lam_vecs, slopes, kconst, qconst, gain, lam_init):
    steps = SEQ // (AT_TQ * AT_NB)
    return pl.pallas_call(
        functools.partial(_attn_kernel, lam_init),
        grid=(HEADS, BATCH, SEQ // (AT_TQ * AT_NB)),
        in_specs=[
            _resident((4, DH)),
            pl.BlockSpec((1, 1, AT_TQ), lambda h, b, i: (h, 0, 0)),
            pl.BlockSpec((1, 1, HEAD_W), lambda h, b, i: (h, 0, 0)),
            pl.BlockSpec((1, 1, HEAD_W), lambda h, b, i: (h, 0, 0)),
            pl.BlockSpec((None, None, AT_TQ * AT_NB, HEAD_W), lambda h, b, i: (0, h, b * steps + i, 0)),
            pl.BlockSpec((None, None, SEQ, HEAD_W), lambda h, b, i: (1, h, b, 0)),
            pl.BlockSpec((None, None, SEQ, HEAD_W), lambda h, b, i: (2, h, b, 0)),
            _resident((1, HEAD_W)),
        ],
        out_specs=pl.BlockSpec((1, AT_TQ * AT_NB, HEAD_W), lambda h, b, i: (b, i, h)),
        out_shape=jax.ShapeDtypeStruct((BATCH, SEQ, HEADS * HEAD_W), BF16),
        scratch_shapes=[
            pltpu.VMEM((SEQ, HEAD_W), F32),
            pltpu.VMEM((AT_TQ, HEAD_W), F32),
            pltpu.VMEM((AT_TQ, AT_TQ), F32),
            pltpu.VMEM((2, SEQ, HEAD_W), BF16),
            pltpu.VMEM((2, SEQ, HEAD_W), BF16),
            pltpu.VMEM((AT_NKB, HEAD_W, AT_KT), BF16),
        ] + AT_NB * [
            pltpu.VMEM((HEAD_W, SEQ), BF16),
            pltpu.VMEM((SEQ, AT_TQ), F32),
            pltpu.VMEM((SEQ, AT_TQ), F32),
            pltpu.VMEM((SEQ, AT_TQ), BF16),
            pltpu.VMEM((SEQ, AT_TQ), BF16),
        ],
        compiler_params=_params(("arbitrary", "arbitrary", "arbitrary")),
        name="diff_attention",
    )(lam_vecs, slopes, kconst, qconst, qkv, qkv, qkv, gain)


SG_TS = 512
SG_PC = 256


def _sgu_kernel(su_ref, sv_ref, gl_ref, bg_ref, lng_ref, lnb_ref, ws_ref, bs_ref, wp_ref,
                out_ref, u_scr, gv_scr, sg_scr):
    for n in range(SG_TS // CHUNK):
        rows = slice(n * CHUNK, (n + 1) * CHUNK)
        v = _gelu(sv_ref[rows, :].astype(F32))
        mu = jnp.mean(v, axis=-1, keepdims=True)
        vc = v - mu
        var = jnp.mean(vc * vc, axis=-1, keepdims=True)
        gv_scr[rows, :] = (vc * lax.rsqrt(var + EPS) * lng_ref[...] + lnb_ref[...]).astype(BF16)
        u_scr[rows, :] = _gelu(su_ref[rows, :].astype(F32))
        for g in range(SGU_GROUPS):
            cols = slice(g * LANES, (g + 1) * LANES)
            mixed = jnp.dot(ws_ref[g], gv_scr[rows, cols], preferred_element_type=F32) + bs_ref[g]
            sg_scr[rows, cols] = (u_scr[rows, cols] * mixed).astype(BF16)

    for r in range(SG_TS // SG_PC):
        rows = slice(r * SG_PC, (r + 1) * SG_PC)
        y = jnp.dot(sg_scr[rows, :], wp_ref[...], preferred_element_type=F32)
        gate = _sigmoid(gl_ref[rows, :].astype(F32) + bg_ref[...])
        out_ref[rows, :] = (gate * y).astype(BF16)


def _sgu_branch(z2, bg, lng, lnb, ws, bs, wp, layer):
    def col(c):
        return pl.BlockSpec((SG_TS, D_MODEL), lambda i: (i, c))

    return pl.pallas_call(
        _sgu_kernel,
        grid=(TOKENS // SG_TS,),
        in_specs=[
            col(Z_SU // D_MODEL), col(Z_SV // D_MODEL), col(Z_GC // D_MODEL),
            _resident((1, D_MODEL)),
            _resident((1, D_MODEL)),
            _resident((1, D_MODEL)),
            _layer_weight((SGU_GROUPS, CHUNK, CHUNK), layer),
            _resident((SGU_GROUPS, CHUNK, LANES)),
            _layer_weight((D_MODEL, D_MODEL), layer),
        ],
        out_specs=pl.BlockSpec((SG_TS, D_MODEL), lambda i: (i, 0)),
        out_shape=jax.ShapeDtypeStruct((TOKENS, D_MODEL), BF16),
        scratch_shapes=[
            pltpu.VMEM((SG_TS, D_MODEL), F32),
            pltpu.VMEM((SG_TS, D_MODEL), BF16),
            pltpu.VMEM((SG_TS, D_MODEL), BF16),
        ],
        compiler_params=_params(("parallel",)),
        name="sgu_branch",
    )(z2, z2, z2, bg, lng, lnb, ws, bs, wp)


MG_TM = 1024


MG_RC = 256


def _merge_kernel(x_ref, ya_ref, o_ref, yc_ref, gl_ref, bg_ref, wa_ref, wout_ref, gpost_ref, out_ref):
    for r in range(MG_TM // MG_RC):
        rows = slice(r * MG_RC, (r + 1) * MG_RC)
        yb = jnp.dot(o_ref[rows, :], wa_ref[...], preferred_element_type=F32)
        gate = _sigmoid(gl_ref[rows, :].astype(F32) + bg_ref[...])
        mix = ya_ref[rows, :].astype(F32) + gate * yb + yc_ref[rows, :].astype(F32)
        m = jnp.dot(mix.astype(BF16), wout_ref[...], preferred_element_type=F32)
        out_ref[rows, :] = x_ref[rows, :] + _rms(m, gpost_ref[...])


def _merge(x2, ya, o, yc, z2, bg, wa, wout, gpost, layer):
    def rows(width=D_MODEL, col=0):
        return pl.BlockSpec((MG_TM, width), lambda i: (i, col))

    return pl.pallas_call(
        _merge_kernel,
        grid=(TOKENS // MG_TM,),
        in_specs=[
            rows(), rows(), rows(), rows(), rows(col=Z_GB // D_MODEL),
            _resident((1, D_MODEL)),
            _layer_weight((D_MODEL, D_MODEL), layer),
            _layer_weight((D_MODEL, D_MODEL), layer),
            _resident((1, D_MODEL)),
        ],
        out_specs=rows(),
        out_shape=jax.ShapeDtypeStruct((TOKENS, D_MODEL), F32),
        compiler_params=_params(("parallel",)),
        name="merge",
    )(x2, ya, o, yc, z2, bg, wa, wout, gpost)


FF_TM = 1024
FF_RC = 512
FF_TC = 1024


def _ffn_kernel(x_ref, gpre_ref, gpost_ref, wup_ref, wdn_ref, out_ref):
    for r in range(FF_TM // FF_RC):
        rows = slice(r * FF_RC, (r + 1) * FF_RC)
        x = x_ref[rows, :]
        h = _rms(x, gpre_ref[...]).astype(BF16)
        acc = jnp.zeros((FF_RC, D_MODEL), F32)
        for c in range(D_FF // FF_TC):
            cols = slice(c * FF_TC, (c + 1) * FF_TC)
            u = jnp.dot(h, wup_ref[:, cols], preferred_element_type=F32)
            u = jnp.square(jnp.maximum(u, 0.0)).astype(BF16)
            acc = acc + jnp.dot(u, wdn_ref[cols, :], preferred_element_type=F32)
        out_ref[rows, :] = x + _rms(acc, gpost_ref[...])


def _ffn(x2, gpre, gpost, wup, wdn, layer):
    return pl.pallas_call(
        _ffn_kernel,
        grid=(TOKENS // FF_TM,),
        in_specs=[
            pl.BlockSpec((FF_TM, D_MODEL), lambda i: (i, 0)),
            _resident((1, D_MODEL)),
            _resident((1, D_MODEL)),
            _layer_weight((D_MODEL, D_FF), layer),
            _layer_weight((D_FF, D_MODEL), layer),
        ],
        out_specs=pl.BlockSpec((FF_TM, D_MODEL), lambda i: (i, 0)),
        out_shape=jax.ShapeDtypeStruct((TOKENS, D_MODEL), F32),
        compiler_params=_params(("parallel",)),
        name="ffn",
    )(x2, gpre, gpost, wup, wdn)


def _row(v):
    return v.reshape(1, -1).astype(F32)


def kernel(x, norm_mix_pre, norm_mix_post, w_in, b_gate, conv_w, conv_b, conv_ln_g, conv_ln_b,
           lam_q1, lam_k1, lam_q2, lam_k2, subln_g, sgu_ln_g, sgu_ln_b, sgu_w, sgu_b,
           w_proj_conv, w_proj_attn, w_proj_sgu, w_out, norm_ffn_pre, norm_ffn_post,
           w_ffn_up, w_ffn_down):
    assert x.shape == (BATCH, SEQ, D_MODEL) and w_in.shape == (DEPTH, D_MODEL, Z_COLS)
    x2 = x.reshape(TOKENS, D_MODEL).astype(F32)

    colscale = jnp.ones((1, Z_COLS), F32).at[:, Z_Q:Z_Q + D_MODEL].set(DH ** -0.5 * LOG2E)
    slopes = jnp.exp2(-8.0 * jnp.arange(1, HEADS + 1, dtype=F32) / HEADS) * LOG2E
    c1 = slopes.astype(BF16).astype(F32)
    c2 = (slopes - c1).astype(BF16).astype(F32)
    c3 = (slopes - c1 - c2).astype(BF16).astype(F32)
    pieces = jnp.stack([c1, c2, c3], axis=1)
    a = jnp.arange(HEAD_W) & (DH - 1)
    kconst = jnp.where(a < 6, -pieces[:, a % 3], 0.0)[:, None, :]
    qconst = jnp.where((a >= 6) & (a < 12), pieces[:, a % 3], 0.0)[:, None, :]
    slopes = jnp.broadcast_to(slopes[:, None, None], (HEADS, 1, AT_TQ))

    w_in_b, w_conv_b, w_attn_b, w_sgu_b, w_out_b, sgu_w_b, w_up_b, w_dn_b = (
        w.astype(BF16) for w in (w_in, w_proj_conv, w_proj_attn, w_proj_sgu, w_out, sgu_w,
                                 w_ffn_up, w_ffn_down))

    for l in range(DEPTH):
        lam_init = 0.8 - 0.6 * math.exp(-0.3 * l)
        cw = conv_w[l].reshape(CONV_K, CV_NC, LANES).transpose(1, 0, 2).astype(F32)
        cb = conv_b[l].reshape(CV_NC, 1, LANES).astype(F32)
        z2, conv_raw, qkv = _proj_in(x2, _row(norm_mix_pre[l]), colscale, w_in_b, l, cw, cb)

        ya = _conv_branch(conv_raw, z2, _row(b_gate[l, 0:D_MODEL]),
                          _row(conv_ln_g[l]), _row(conv_ln_b[l]), w_conv_b, l)

        lam_vecs = jnp.stack([lam_q1[l], lam_q2[l], lam_k1[l], lam_k2[l]]).astype(F32)
        o = _attention(qkv, lam_vecs, slopes, kconst, qconst,
                       _row(subln_g[l]) * (1.0 - lam_init), lam_init)

        bs = jnp.broadcast_to(sgu_b[l].astype(F32)[:, :, None], (SGU_GROUPS, CHUNK, LANES))
        yc = _sgu_branch(z2, _row(b_gate[l, 2 * D_MODEL:3 * D_MODEL]), _row(sgu_ln_g[l]),
                         _row(sgu_ln_b[l]), sgu_w_b, bs, w_sgu_b, l)

        x2 = _merge(x2, ya, o.reshape(TOKENS, D_MODEL), yc, z2,
                    _row(b_gate[l, D_MODEL:2 * D_MODEL]), w_attn_b, w_out_b,
                    _row(norm_mix_post[l]), l)

        x2 = _ffn(x2, _row(norm_ffn_pre[l]), _row(norm_ffn_post[l]), w_up_b, w_dn_b, l)

    return x2.reshape(BATCH, SEQ, D_MODEL).astype(x.dtype)
```

```python
import functools
import math

import jax
import jax.numpy as jnp
from jax import lax
from jax.experimental import pallas as pl
from jax.experimental.pallas import tpu as pltpu

F32 = jnp.float32
BF16 = jnp.bfloat16

D_MODEL = 1024
BATCH = 8
SEQ = 2048
DEPTH = 2
TOKENS = BATCH * SEQ
EPS = 1e-6

CONV_K = 31
CONV_PAD = (CONV_K - 1) // 2
HEADS = 8
DH = 64
HEAD_W = 2 * DH
SGU_GROUPS = 8
CHUNK = 128
D_FF = 4 * D_MODEL

Z_A, Z_B, Z_Q, Z_K, Z_V, Z_SU, Z_SV, Z_GA, Z_GB, Z_GC = (i * D_MODEL for i in range(10))
Z_COLS = 10 * D_MODEL

LANES = 128
SUBLANES = 8
HALO = 16
LOG2E = 1.4426950408889634

VMEM_LIMIT = 56 * 1024 * 1024


def _params(semantics):
    return pltpu.CompilerParams(dimension_semantics=semantics, vmem_limit_bytes=VMEM_LIMIT)


def _resident(shape):
    zeros = (0,) * len(shape)
    return pl.BlockSpec(shape, lambda *_: zeros, pipeline_mode=pl.Buffered(1))


def _layer_weight(shape, layer):
    zeros = (0,) * len(shape)
    return pl.BlockSpec((None,) + shape, lambda *_: (layer,) + zeros, pipeline_mode=pl.Buffered(1))


def _rms(x, g):
    ms = jnp.mean(x * x, axis=-1, keepdims=True)
    return x * lax.rsqrt(ms + EPS) * g


def _sigmoid(x):
    return 0.5 * jnp.tanh(0.5 * x) + 0.5


def _gelu(x):
    return 0.5 * x * (1.0 + lax.erf(x * (2.0 ** -0.5)))


PI_TM = SEQ
PI_TN = 1024
PI_RC = 256
PI_CONV_J0 = 2
CV_R = 128
CV_NC = D_MODEL // LANES
assert Z_A == 0 and Z_B == PI_TN and Z_COLS // PI_TN == PI_CONV_J0 + CV_NC


def _proj_in_kernel(x_ref, g_ref, cs_ref, w_ref, cw_ref, cb_ref, z_ref, cv_ref, h_ref, zab_scr, a_scr):
    j = pl.program_id(1)

    @pl.when(j == 0)
    def _():
        for r in range(PI_TM // PI_RC):
            rows = slice(r * PI_RC, (r + 1) * PI_RC)
            h_ref[rows, :] = _rms(x_ref[rows, :], g_ref[...]).astype(BF16)

    def project(r, keep):
        rows = slice(r * PI_RC, (r + 1) * PI_RC)
        acc = jnp.dot(h_ref[rows, :], w_ref[...], preferred_element_type=F32)
        zb = (acc * cs_ref[...]).astype(BF16)
        z_ref[rows, :] = zb
        if keep:
            for c in range(CV_NC):
                zab_scr[j, c, rows, :] = zb[:, c * LANES:(c + 1) * LANES]

    @pl.when(j < PI_CONV_J0)
    def _():
        for r in range(PI_TM // PI_RC):
            project(r, keep=True)

    @pl.when(j >= PI_CONV_J0)
    def _():
        c = j - PI_CONV_J0
        zero = jnp.zeros((HALO, LANES), F32)
        a_scr[0:HALO, :] = zero
        a_scr[HALO + SEQ:, :] = zero
        for r in range(SEQ // PI_RC):
            rows = slice(r * PI_RC, (r + 1) * PI_RC)
            glu = zab_scr[0, c, rows, :].astype(F32) * _sigmoid(zab_scr[1, c, rows, :].astype(F32))
            a_scr[HALO + r * PI_RC:HALO + (r + 1) * PI_RC, :] = glu
        for r in range(PI_TM // PI_RC):
            project(r, keep=False)
            for rr in range(r * (PI_RC // CV_R), (r + 1) * (PI_RC // CV_R)):
                acc = jnp.zeros((CV_R, LANES), F32)
                for t in range(CONV_K):
                    start = rr * CV_R + HALO - CONV_PAD + t
                    acc = acc + cw_ref[0, t:t + 1, :] * a_scr[start:start + CV_R, :]
                cv_ref[rr * CV_R:(rr + 1) * CV_R, :] = acc + cb_ref[0]


def _proj_in(x2, g, colscale, w, layer, cw, cb):
    def chunk(i, j):
        return jnp.maximum(j - PI_CONV_J0, 0)

    return pl.pallas_call(
        _proj_in_kernel,
        grid=(TOKENS // PI_TM, Z_COLS // PI_TN),
        in_specs=[
            pl.BlockSpec((PI_TM, D_MODEL), lambda i, j: (i, 0)),
            _resident((1, D_MODEL)),
            pl.BlockSpec((1, PI_TN), lambda i, j: (0, j)),
            pl.BlockSpec((None, D_MODEL, PI_TN), lambda i, j: (layer, 0, j)),
            pl.BlockSpec((1, CONV_K, LANES), lambda i, j: (chunk(i, j), 0, 0)),
            pl.BlockSpec((1, 1, LANES), lambda i, j: (chunk(i, j), 0, 0)),
        ],
        out_specs=[
            pl.BlockSpec((PI_TM, PI_TN), lambda i, j: (i, j)),
            pl.BlockSpec((PI_TM, LANES), lambda i, j: (i, chunk(i, j))),
        ],
        out_shape=[
            jax.ShapeDtypeStruct((TOKENS, Z_COLS), BF16),
            jax.ShapeDtypeStruct((TOKENS, D_MODEL), F32),
        ],
        scratch_shapes=[
            pltpu.VMEM((PI_TM, D_MODEL), BF16),
            pltpu.VMEM((PI_CONV_J0, CV_NC, SEQ, LANES), BF16),
            pltpu.VMEM((SEQ + 2 * HALO, LANES), F32),
        ],
        compiler_params=_params(("parallel", "arbitrary")),
        name="proj_in",
    )(x2, g, colscale, w, cw, cb)


CV_TS = 1024
CV_RC = 128
CV_PC = 256


def _conv_kernel(cv_ref, gl_ref, bg_ref, lng_ref, lnb_ref, wp_ref, out_ref, act_scr):
    for r in range(CV_TS // CV_RC):
        rows = slice(r * CV_RC, (r + 1) * CV_RC)
        c = cv_ref[rows, :]
        mu = jnp.mean(c, axis=-1, keepdims=True)
        d = c - mu
        var = jnp.mean(d * d, axis=-1, keepdims=True)
        y = d * lax.rsqrt(var + EPS) * lng_ref[...] + lnb_ref[...]
        act_scr[rows, :] = (y * _sigmoid(y)).astype(BF16)

    for r in range(CV_TS // CV_PC):
        rows = slice(r * CV_PC, (r + 1) * CV_PC)
        y = jnp.dot(act_scr[rows, :], wp_ref[...], preferred_element_type=F32)
        gate = _sigmoid(gl_ref[rows, :].astype(F32) + bg_ref[...])
        out_ref[rows, :] = (gate * y).astype(BF16)


def _conv_branch(conv_raw, z2, bg, lng, lnb, wp, layer):
    return pl.pallas_call(
        _conv_kernel,
        grid=(TOKENS // CV_TS,),
        in_specs=[
            pl.BlockSpec((CV_TS, D_MODEL), lambda i: (i, 0)),
            pl.BlockSpec((CV_TS, D_MODEL), lambda i: (i, Z_GA // D_MODEL)),
            _resident((1, D_MODEL)),
            _resident((1, D_MODEL)),
            _resident((1, D_MODEL)),
            _layer_weight((D_MODEL, D_MODEL), layer),
        ],
        out_specs=pl.BlockSpec((CV_TS, D_MODEL), lambda i: (i, 0)),
        out_shape=jax.ShapeDtypeStruct((TOKENS, D_MODEL), BF16),
        scratch_shapes=[pltpu.VMEM((CV_TS, D_MODEL), BF16)],
        compiler_params=_params(("parallel",)),
        name="conv_branch",
    )(conv_raw, z2, bg, lng, lnb, wp)


AT_TQ = 256
AT_NB = 4
AT_RB = 64
AT_KT = 256
AT_NKB = SEQ // AT_KT
AT_NKB_LOG2 = AT_NKB.bit_length() - 1
AT_BLK_SCR = 5
assert AT_NKB == 1 << AT_NKB_LOG2 and AT_KT == AT_TQ


def _score_softmax_pass(score_jobs, softmax_jobs, corr_ref):
    accs = [None] * len(score_jobs)
    sums = [jnp.zeros((SUBLANES, AT_TQ), F32)] * len(softmax_jobs)
    tiled = (AT_RB // SUBLANES, SUBLANES, AT_TQ)
    for t in range(AT_NKB):
        for i, (qk, s_ref) in enumerate(score_jobs):
            s = qk(t)
            for r in range(AT_KT // AT_RB):
                tile = s[r * AT_RB:(r + 1) * AT_RB, :]
                if t == 0:
                    tile = tile + corr_ref[r * AT_RB:(r + 1) * AT_RB, :]
                s_ref[t * AT_KT + r * AT_RB:t * AT_KT + (r + 1) * AT_RB, :] = tile
                part = jnp.max(tile.reshape(tiled), axis=0)
                accs[i] = part if accs[i] is None else jnp.maximum(accs[i], part)
        for i, (s_ref, m, p_ref) in enumerate(softmax_jobs):
            for r in range(AT_KT // AT_RB):
                rows = slice(t * AT_KT + r * AT_RB, t * AT_KT + (r + 1) * AT_RB)
                p = jnp.exp2(s_ref[rows, :].reshape(tiled) - m[None])
                sums[i] = sums[i] + jnp.sum(p, axis=0)
                p_ref[rows, :] = p.reshape(AT_RB, AT_TQ).astype(BF16)
    maxes = [jnp.broadcast_to(jnp.max(a, axis=0, keepdims=True), (SUBLANES, AT_TQ)) for a in accs]
    return maxes, [jnp.sum(x, axis=0, keepdims=True) for x in sums]


def _attn_kernel(lam_init, lam_ref, slope_ref, kc_ref, qc_ref, q_ref, k_ref, v_ref, g_ref, o_ref,
                 kaug_scr, qbase_scr, corr_scr, klr0_scr, klr1_scr, vt_scr, *per_block):
    klr_scr = (klr0_scr, klr1_scr)
    blk_scr = [per_block[AT_BLK_SCR * n:AT_BLK_SCR * (n + 1)] for n in range(AT_NB)]
    b = pl.program_id(1)
    step = pl.program_id(2)
    lane1 = lax.broadcasted_iota(jnp.int32, (1, HEAD_W), 1) & (DH - 1)

    @pl.when((b == 0) & (step == 0))
    def _():
        is_jl = ((lane1 >= 6) & (lane1 < 9)).astype(F32)
        is_jh = ((lane1 >= 9) & (lane1 < 12)).astype(F32)
        is_il = (lane1 < 3).astype(F32)
        row = lax.broadcasted_iota(jnp.int32, (SEQ, HEAD_W), 0)
        jl = (row & (AT_TQ - 1)).astype(F32)
        jh = (row - (row & (AT_TQ - 1))).astype(F32)
        kaug_scr[...] = kc_ref[0] + is_jl * jl + is_jh * jh
        il = lax.broadcasted_iota(jnp.int32, (AT_TQ, HEAD_W), 0).astype(F32)
        qbase_scr[...] = qc_ref[0] + is_il * il
        kk = lax.broadcasted_iota(jnp.int32, (AT_TQ, AT_TQ), 0)
        qq = lax.broadcasted_iota(jnp.int32, (AT_TQ, AT_TQ), 1)
        corr_scr[...] = -2.0 * slope_ref[0] * jnp.maximum(kk - qq, 0).astype(F32)

    @pl.when(step == 0)
    def _():
        half_k = lax.broadcasted_iota(jnp.int32, (SEQ, HEAD_W), 1) >> 6
        kf = k_ref[0].astype(F32)
        kaug = kaug_scr[...]
        for mp in range(2):
            klr_scr[mp][0] = jnp.where(half_k == mp, kf, kaug).astype(BF16)
            klr_scr[mp][1] = jnp.where(half_k == mp, kf, -kaug).astype(BF16)
        for j in range(AT_NKB):
            vt_scr[j] = v_ref[0, j * AT_KT:(j + 1) * AT_KT, :].astype(F32).T.astype(BF16)

    lam_v = lam_ref[...]
    dots = jnp.sum(lam_v[0:2, :] * lam_v[2:4, :], axis=-1, keepdims=True)
    e = jnp.exp(dots)
    lam = e[0:1, :] - e[1:2, :] + lam_init

    is_q0 = ((lane1 >= 3) & (lane1 < 6)).astype(F32)
    half_q = lax.broadcasted_iota(jnp.int32, (AT_TQ, HEAD_W), 1) >> 6

    def qk_stage(n):
        vtr_scr, s_scr = blk_scr[n][0], blk_scr[n][1:3]
        qi = step * AT_NB + n
        kblk = [(qi + t) & (AT_NKB - 1) for t in range(AT_NKB)]
        for t in range(AT_NKB):
            vtr_scr[:, t * AT_KT:(t + 1) * AT_KT] = vt_scr[kblk[t]]
        qf = q_ref[0, n * AT_TQ:(n + 1) * AT_TQ, :].astype(F32)
        qaug = qbase_scr[...] + is_q0 * (qi * AT_TQ).astype(F32)
        jobs = []
        for mp in range(2):
            qat = jnp.where(half_q == mp, qf, qaug).T.astype(BF16)

            def qk(t, mp=mp, qat=qat):
                side = 0 if t == 0 else 1 - ((qi + t) >> AT_NKB_LOG2)
                src = pl.ds(pl.multiple_of(kblk[t] * AT_KT, AT_KT), AT_KT)
                return jnp.dot(klr_scr[mp][side, src, :], qat, preferred_element_type=F32)

            jobs.append((qk, s_scr[mp]))
        return jobs

    def softmax_jobs(n, mx):
        s_scr, p_scr = blk_scr[n][1:3], blk_scr[n][3:5]
        return [(s_scr[mp], mx[mp], p_scr[mp]) for mp in range(2)]

    def pv_stage(n, denom):
        vtr_scr, p_scr = blk_scr[n][0], blk_scr[n][3:5]
        outs = [jnp.dot(vtr_scr[...], p_scr[mp][...], preferred_element_type=F32) / denom[mp]
                for mp in range(2)]
        ot = outs[0] - lam * outs[1]
        ms = jnp.mean(ot * ot, axis=0, keepdims=True)
        ot = ot * lax.rsqrt(ms + EPS)
        o_ref[0, n * AT_TQ:(n + 1) * AT_TQ, :] = (ot.T * g_ref[...]).astype(BF16)

    mx, _ = _score_softmax_pass(qk_stage(0), [], corr_scr)
    for n in range(AT_NB):
        nxt = qk_stage(n + 1) if n + 1 < AT_NB else []
        mx, denom = _score_softmax_pass(nxt, softmax_jobs(n, mx), corr_scr)
        pv_stage(n, denom)


def _attention(z3, lam_vecs, slopes, kconst, qconst, gain, lam_init):
    qc, kc, vc = Z_Q // HEAD_W, Z_K // HEAD_W, Z_V // HEAD_W
    return pl.pallas_call(
        functools.partial(_attn_kernel, lam_init),
        grid=(HEADS, BATCH, SEQ // (AT_TQ * AT_NB)),
        in_specs=[
            _resident((4, DH)),
            pl.BlockSpec((1, 1, AT_TQ), lambda h, b, i: (h, 0, 0)),
            pl.BlockSpec((1, 1, HEAD_W), lambda h, b, i: (h, 0, 0)),
            pl.BlockSpec((1, 1, HEAD_W), lambda h, b, i: (h, 0, 0)),
            pl.BlockSpec((1, AT_TQ * AT_NB, HEAD_W), lambda h, b, i: (b, i, qc + h)),
            pl.BlockSpec((1, SEQ, HEAD_W), lambda h, b, i: (b, 0, kc + h)),
            pl.BlockSpec((1, SEQ, HEAD_W), lambda h, b, i: (b, 0, vc + h)),
            _resident((1, HEAD_W)),
        ],
        out_specs=pl.BlockSpec((1, AT_TQ * AT_NB, HEAD_W), lambda h, b, i: (b, i, h)),
        out_shape=jax.ShapeDtypeStruct((BATCH, SEQ, HEADS * HEAD_W), BF16),
        scratch_shapes=[
            pltpu.VMEM((SEQ, HEAD_W), F32),
            pltpu.VMEM((AT_TQ, HEAD_W), F32),
            pltpu.VMEM((AT_TQ, AT_TQ), F32),
            pltpu.VMEM((2, SEQ, HEAD_W), BF16),
            pltpu.VMEM((2, SEQ, HEAD_W), BF16),
            pltpu.VMEM((AT_NKB, HEAD_W, AT_KT), BF16),
        ] + AT_NB * [
            pltpu.VMEM((HEAD_W, SEQ), BF16),
            pltpu.VMEM((SEQ, AT_TQ), F32),
            pltpu.VMEM((SEQ, AT_TQ), F32),
            pltpu.VMEM((SEQ, AT_TQ), BF16),
            pltpu.VMEM((SEQ, AT_TQ), BF16),
        ],
        compiler_params=_params(("arbitrary", "arbitrary", "arbitrary")),
        name="diff_attention",
    )(lam_vecs, slopes, kconst, qconst, z3, z3, z3, gain)


SG_TS = 512
SG_PC = 256


def _sgu_kernel(su_ref, sv_ref, gl_ref, bg_ref, lng_ref, lnb_ref, ws_ref, bs_ref, wp_ref,
                out_ref, u_scr, gv_scr, sg_scr):
    for n in range(SG_TS // CHUNK):
        rows = slice(n * CHUNK, (n + 1) * CHUNK)
        v = _gelu(sv_ref[rows, :].astype(F32))
        mu = jnp.mean(v, axis=-1, keepdims=True)
        vc = v - mu
        var = jnp.mean(vc * vc, axis=-1, keepdims=True)
        gv_scr[rows, :] = (vc * lax.rsqrt(var + EPS) * lng_ref[...] + lnb_ref[...]).astype(BF16)
        u_scr[rows, :] = _gelu(su_ref[rows, :].astype(F32))
        for g in range(SGU_GROUPS):
            cols = slice(g * LANES, (g + 1) * LANES)
            mixed = jnp.dot(ws_ref[g], gv_scr[rows, cols], preferred_element_type=F32) + bs_ref[g]
            sg_scr[rows, cols] = (u_scr[rows, cols] * mixed).astype(BF16)

    for r in range(SG_TS // SG_PC):
        rows = slice(r * SG_PC, (r + 1) * SG_PC)
        y = jnp.dot(sg_scr[rows, :], wp_ref[...], preferred_element_type=F32)
        gate = _sigmoid(gl_ref[rows, :].astype(F32) + bg_ref[...])
        out_ref[rows, :] = (gate * y).astype(BF16)


def _sgu_branch(z2, bg, lng, lnb, ws, bs, wp, layer):
    def col(c):
        return pl.BlockSpec((SG_TS, D_MODEL), lambda i: (i, c))

    return pl.pallas_call(
        _sgu_kernel,
        grid=(TOKENS // SG_TS,),
        in_specs=[
            col(Z_SU // D_MODEL), col(Z_SV // D_MODEL), col(Z_GC // D_MODEL),
            _resident((1, D_MODEL)),
            _resident((1, D_MODEL)),
            _resident((1, D_MODEL)),
            _layer_weight((SGU_GROUPS, CHUNK, CHUNK), layer),
            _resident((SGU_GROUPS, CHUNK, LANES)),
            _layer_weight((D_MODEL, D_MODEL), layer),
        ],
        out_specs=pl.BlockSpec((SG_TS, D_MODEL), lambda i: (i, 0)),
        out_shape=jax.ShapeDtypeStruct((TOKENS, D_MODEL), BF16),
        scratch_shapes=[
            pltpu.VMEM((SG_TS, D_MODEL), F32),
            pltpu.VMEM((SG_TS, D_MODEL), BF16),
            pltpu.VMEM((SG_TS, D_MODEL), BF16),
        ],
        compiler_params=_params(("parallel",)),
        name="sgu_branch",
    )(z2, z2, z2, bg, lng, lnb, ws, bs, wp)


MG_TM = 1024
MG_RC = 256


def _merge_kernel(x_ref, ya_ref, o_ref, yc_ref, gl_ref, bg_ref, wa_ref, wout_ref, gpost_ref, out_ref):
    for r in range(MG_TM // MG_RC):
        rows = slice(r * MG_RC, (r + 1) * MG_RC)
        yb = jnp.dot(o_ref[rows, :], wa_ref[...], preferred_element_type=F32)
        gate = _sigmoid(gl_ref[rows, :].astype(F32) + bg_ref[...])
        mix = ya_ref[rows, :].astype(F32) + gate * yb + yc_ref[rows, :].astype(F32)
        m = jnp.dot(mix.astype(BF16), wout_ref[...], preferred_element_type=F32)
        out_ref[rows, :] = x_ref[rows, :] + _rms(m, gpost_ref[...])


def _merge(x2, ya, o, yc, z2, bg, wa, wout, gpost, layer):
    def rows(width=D_MODEL, col=0):
        return pl.BlockSpec((MG_TM, width), lambda i: (i, col))

    return pl.pallas_call(
        _merge_kernel,
        grid=(TOKENS // MG_TM,),
        in_specs=[
            rows(), rows(), rows(), rows(), rows(col=Z_GB // D_MODEL),
            _resident((1, D_MODEL)),
            _layer_weight((D_MODEL, D_MODEL), layer),
            _layer_weight((D_MODEL, D_MODEL), layer),
            _resident((1, D_MODEL)),
        ],
        out_specs=rows(),
        out_shape=jax.ShapeDtypeStruct((TOKENS, D_MODEL), F32),
        compiler_params=_params(("parallel",)),
        name="merge",
    )(x2, ya, o, yc, z2, bg, wa, wout, gpost)


FF_TM = 1024
FF_RC = 512
FF_TC = 1024


def _ffn_kernel(x_ref, gpre_ref, gpost_ref, wup_ref, wdn_ref, out_ref):
    for r in range(FF_TM // FF_RC):
        rows = slice(r * FF_RC, (r + 1) * FF_RC)
        x = x_ref[rows, :]
        h = _rms(x, gpre_ref[...]).astype(BF16)
        acc = jnp.zeros((FF_RC, D_MODEL), F32)
        for c in range(D_FF // FF_TC):
            cols = slice(c * FF_TC, (c + 1) * FF_TC)
            u = jnp.dot(h, wup_ref[:, cols], preferred_element_type=F32)
            u = jnp.square(jnp.maximum(u, 0.0)).astype(BF16)
            acc = acc + jnp.dot(u, wdn_ref[cols, :], preferred_element_type=F32)
        out_ref[rows, :] = x + _rms(acc, gpost_ref[...])


def _ffn(x2, gpre, gpost, wup, wdn, layer):
    return pl.pallas_call(
        _ffn_kernel,
        grid=(TOKENS // FF_TM,),
        in_specs=[
            pl.BlockSpec((FF_TM, D_MODEL), lambda i: (i, 0)),
            _resident((1, D_MODEL)),
            _resident((1, D_MODEL)),
            _layer_weight((D_MODEL, D_FF), layer),
            _layer_weight((D_FF, D_MODEL), layer),
        ],
        out_specs=pl.BlockSpec((FF_TM, D_MODEL), lambda i: (i, 0)),
        out_shape=jax.ShapeDtypeStruct((TOKENS, D_MODEL), F32),
        compiler_params=_params(("parallel",)),
        name="ffn",
    )(x2, gpre, gpost, wup, wdn)


def _row(v):
    return v.reshape(1, -1).astype(F32)


def kernel(x, norm_mix_pre, norm_mix_post, w_in, b_gate, conv_w, conv_b, conv_ln_g, conv_ln_b,
           lam_q1, lam_k1, lam_q2, lam_k2, subln_g, sgu_ln_g, sgu_ln_b, sgu_w, sgu_b,
           w_proj_conv, w_proj_attn, w_proj_sgu, w_out, norm_ffn_pre, norm_ffn_post,
           w_ffn_up, w_ffn_down):
    assert x.shape == (BATCH, SEQ, D_MODEL) and w_in.shape == (DEPTH, D_MODEL, Z_COLS)
    x2 = x.reshape(TOKENS, D_MODEL).astype(F32)

    colscale = jnp.ones((1, Z_COLS), F32).at[:, Z_Q:Z_Q + D_MODEL].set(DH ** -0.5 * LOG2E)
    slopes = jnp.exp2(-8.0 * jnp.arange(1, HEADS + 1, dtype=F32) / HEADS) * LOG2E
    c1 = slopes.astype(BF16).astype(F32)
    c2 = (slopes - c1).astype(BF16).astype(F32)
    c3 = (slopes - c1 - c2).astype(BF16).astype(F32)
    pieces = jnp.stack([c1, c2, c3], axis=1)
    a = jnp.arange(HEAD_W) & (DH - 1)
    kconst = jnp.where(a < 6, -pieces[:, a % 3], 0.0)[:, None, :]
    qconst = jnp.where((a >= 6) & (a < 12), pieces[:, a % 3], 0.0)[:, None, :]
    slopes = jnp.broadcast_to(slopes[:, None, None], (HEADS, 1, AT_TQ))

    w_in_b, w_conv_b, w_attn_b, w_sgu_b, w_out_b, sgu_w_b, w_up_b, w_dn_b = (
        w.astype(BF16) for w in (w_in, w_proj_conv, w_proj_attn, w_proj_sgu, w_out, sgu_w,
                                 w_ffn_up, w_ffn_down))

    for l in range(DEPTH):
        lam_init = 0.8 - 0.6 * math.exp(-0.3 * l)
        cw = conv_w[l].reshape(CONV_K, CV_NC, LANES).transpose(1, 0, 2).astype(F32)
        cb = conv_b[l].reshape(CV_NC, 1, LANES).astype(F32)
        z2, conv_raw = _proj_in(x2, _row(norm_mix_pre[l]), colscale, w_in_b, l, cw, cb)
        z3 = z2.reshape(BATCH, SEQ, Z_COLS)

        ya = _conv_branch(conv_raw, z2, _row(b_gate[l, 0:D_MODEL]),
                          _row(conv_ln_g[l]), _row(conv_ln_b[l]), w_conv_b, l)

        lam_vecs = jnp.stack([lam_q1[l], lam_q2[l], lam_k1[l], lam_k2[l]]).astype(F32)
        o = _attention(z3, lam_vecs, slopes, kconst, qconst,
                       _row(subln_g[l]) * (1.0 - lam_init), lam_init)

        bs = jnp.broadcast_to(sgu_b[l].astype(F32)[:, :, None], (SGU_GROUPS, CHUNK, LANES))
        yc = _sgu_branch(z2, _row(b_gate[l, 2 * D_MODEL:3 * D_MODEL]), _row(sgu_ln_g[l]),
                         _row(sgu_ln_b[l]), sgu_w_b, bs, w_sgu_b, l)

        x2 = _merge(x2, ya, o.reshape(TOKENS, D_MODEL), yc, z2,
                    _row(b_gate[l, D_MODEL:2 * D_MODEL]), w_attn_b, w_out_b,
                    _row(norm_mix_post[l]), l)

        x2 = _ffn(x2, _row(norm_ffn_pre[l]), _row(norm_ffn_post[l]), w_up_b, w_dn_b, l)

    return x2.reshape(BATCH, SEQ, D_MODEL).astype(x.dtype)
```

```python
import functools
import math

import jax
import jax.numpy as jnp
from jax import lax
from jax.experimental import pallas as pl
from jax.experimental.pallas import tpu as pltpu

F32 = jnp.float32
BF16 = jnp.bfloat16

D_MODEL = 1024
BATCH = 8
SEQ = 2048
DEPTH = 2
TOKENS = BATCH * SEQ
EPS = 1e-6

CONV_K = 31
CONV_PAD = (CONV_K - 1) // 2
HEADS = 8
DH = 64
HEAD_W = 2 * DH
SGU_GROUPS = 8
CHUNK = 128
D_FF = 4 * D_MODEL

Z_A, Z_B, Z_Q, Z_K, Z_V, Z_SU, Z_SV, Z_GA, Z_GB, Z_GC = (i * D_MODEL for i in range(10))
Z_COLS = 10 * D_MODEL

LANES = 128
SUBLANES = 8
HALO = 16
LOG2E = 1.4426950408889634

VMEM_LIMIT = 56 * 1024 * 1024


def _params(semantics):
    return pltpu.CompilerParams(dimension_semantics=semantics, vmem_limit_bytes=VMEM_LIMIT)


def _resident(shape):
    zeros = (0,) * len(shape)
    return pl.BlockSpec(shape, lambda *_: zeros, pipeline_mode=pl.Buffered(1))


def _layer_weight(shape, layer):
    zeros = (0,) * len(shape)
    return pl.BlockSpec((None,) + shape, lambda *_: (layer,) + zeros, pipeline_mode=pl.Buffered(1))


def _rms(x, g):
    ms = jnp.mean(x * x, axis=-1, keepdims=True)
    return x * lax.rsqrt(ms + EPS) * g


def _sigmoid(x):
    return 0.5 * jnp.tanh(0.5 * x) + 0.5


def _gelu(x):
    return 0.5 * x * (1.0 + lax.erf(x * (2.0 ** -0.5)))


PI_TM = SEQ
PI_TN = 1024
PI_RC = 256
PI_CONV_J0 = 2
PI_Q_J = Z_Q // PI_TN
assert PI_Q_J >= PI_CONV_J0 and Z_K == Z_Q + PI_TN
CV_R = 128
CV_NC = D_MODEL // LANES
assert Z_A == 0 and Z_B == PI_TN and Z_COLS // PI_TN == PI_CONV_J0 + CV_NC


def _proj_in_kernel(x_ref, g_ref, w_ref, cw_ref, cb_ref, z_ref, cv_ref, h_ref, zab_scr, a_scr):
    j = pl.program_id(1)

    @pl.when(j == 0)
    def _():
        for r in range(PI_TM // PI_RC):
            rows = slice(r * PI_RC, (r + 1) * PI_RC)
            h_ref[rows, :] = _rms(x_ref[rows, :], g_ref[...]).astype(BF16)

    def project(r, keep=False, scale=None):
        rows = slice(r * PI_RC, (r + 1) * PI_RC)
        acc = jnp.dot(h_ref[rows, :], w_ref[...], preferred_element_type=F32)
        zb = (acc if scale is None else acc * scale).astype(BF16)
        z_ref[rows, :] = zb
        if keep:
            for c in range(CV_NC):
                zab_scr[j, c, rows, :] = zb[:, c * LANES:(c + 1) * LANES]

    @pl.when(j < PI_CONV_J0)
    def _():
        for r in range(PI_TM // PI_RC):
            project(r, keep=True)

    def project_and_convolve(scale):
        c = j - PI_CONV_J0
        zero = jnp.zeros((HALO, LANES), F32)
        a_scr[0:HALO, :] = zero
        a_scr[HALO + SEQ:, :] = zero
        for r in range(SEQ // PI_RC):
            rows = slice(r * PI_RC, (r + 1) * PI_RC)
            glu = zab_scr[0, c, rows, :].astype(F32) * _sigmoid(zab_scr[1, c, rows, :].astype(F32))
            a_scr[HALO + r * PI_RC:HALO + (r + 1) * PI_RC, :] = glu
        for r in range(PI_TM // PI_RC):
            project(r, scale=scale)
            for rr in range(r * (PI_RC // CV_R), (r + 1) * (PI_RC // CV_R)):
                acc = jnp.zeros((CV_R, LANES), F32)
                for t in range(CONV_K):
                    start = rr * CV_R + HALO - CONV_PAD + t
                    acc = acc + cw_ref[0, t:t + 1, :] * a_scr[start:start + CV_R, :]
                cv_ref[rr * CV_R:(rr + 1) * CV_R, :] = acc + cb_ref[0]

    @pl.when(j == PI_Q_J)
    def _():
        project_and_convolve(scale=DH ** -0.5 * LOG2E)

    @pl.when((j >= PI_CONV_J0) & (j != PI_Q_J))
    def _():
        project_and_convolve(scale=None)


def _proj_in(x2, g, w, layer, cw, cb):
    def chunk(i, j):
        return jnp.maximum(j - PI_CONV_J0, 0)

    return pl.pallas_call(
        _proj_in_kernel,
        grid=(TOKENS // PI_TM, Z_COLS // PI_TN),
        in_specs=[
            pl.BlockSpec((PI_TM, D_MODEL), lambda i, j: (i, 0)),
            _resident((1, D_MODEL)),
            pl.BlockSpec((None, D_MODEL, PI_TN), lambda i, j: (layer, 0, j)),
            pl.BlockSpec((1, CONV_K, LANES), lambda i, j: (chunk(i, j), 0, 0)),
            pl.BlockSpec((1, 1, LANES), lambda i, j: (chunk(i, j), 0, 0)),
        ],
        out_specs=[
            pl.BlockSpec((PI_TM, PI_TN), lambda i, j: (i, j)),
            pl.BlockSpec((PI_TM, LANES), lambda i, j: (i, chunk(i, j))),
        ],
        out_shape=[
            jax.ShapeDtypeStruct((TOKENS, Z_COLS), BF16),
            jax.ShapeDtypeStruct((TOKENS, D_MODEL), F32),
        ],
        scratch_shapes=[
            pltpu.VMEM((PI_TM, D_MODEL), BF16),
            pltpu.VMEM((PI_CONV_J0, CV_NC, SEQ, LANES), BF16),
            pltpu.VMEM((SEQ + 2 * HALO, LANES), F32),
        ],
        compiler_params=_params(("parallel", "arbitrary")),
        name="proj_in",
    )(x2, g, w, cw, cb)


CV_TS = 1024
CV_RC = 128
CV_PC = 256


def _conv_kernel(cv_ref, gl_ref, bg_ref, lng_ref, lnb_ref, wp_ref, out_ref, act_scr):
    for r in range(CV_TS // CV_RC):
        rows = slice(r * CV_RC, (r + 1) * CV_RC)
        c = cv_ref[rows, :]
        mu = jnp.mean(c, axis=-1, keepdims=True)
        d = c - mu
        var = jnp.mean(d * d, axis=-1, keepdims=True)
        y = d * lax.rsqrt(var + EPS) * lng_ref[...] + lnb_ref[...]
        act_scr[rows, :] = (y * _sigmoid(y)).astype(BF16)

    for r in range(CV_TS // CV_PC):
        rows = slice(r * CV_PC, (r + 1) * CV_PC)
        y = jnp.dot(act_scr[rows, :], wp_ref[...], preferred_element_type=F32)
        gate = _sigmoid(gl_ref[rows, :].astype(F32) + bg_ref[...])
        out_ref[rows, :] = (gate * y).astype(BF16)


def _conv_branch(conv_raw, z2, bg, lng, lnb, wp, layer):
    return pl.pallas_call(
        _conv_kernel,
        grid=(TOKENS // CV_TS,),
        in_specs=[
            pl.BlockSpec((CV_TS, D_MODEL), lambda i: (i, 0)),
            pl.BlockSpec((CV_TS, D_MODEL), lambda i: (i, Z_GA // D_MODEL)),
            _resident((1, D_MODEL)),
            _resident((1, D_MODEL)),
            _resident((1, D_MODEL)),
            _layer_weight((D_MODEL, D_MODEL), layer),
        ],
        out_specs=pl.BlockSpec((CV_TS, D_MODEL), lambda i: (i, 0)),
        out_shape=jax.ShapeDtypeStruct((TOKENS, D_MODEL), BF16),
        scratch_shapes=[pltpu.VMEM((CV_TS, D_MODEL), BF16)],
        compiler_params=_params(("parallel",)),
        name="conv_branch",
    )(conv_raw, z2, bg, lng, lnb, wp)


AT_TQ = 256
AT_NB = 8
AT_NBUF = 2
AT_RB = 64
AT_KT = 256
AT_NKB = SEQ // AT_KT
AT_NKB_LOG2 = AT_NKB.bit_length() - 1
AT_BLK_SCR = 5
assert AT_NKB == 1 << AT_NKB_LOG2 and AT_KT == AT_TQ


def _score_softmax_pass(score_jobs, softmax_jobs, corr_ref):
    accs = [None] * len(score_jobs)
    sums = [jnp.zeros((SUBLANES, AT_TQ), F32)] * len(softmax_jobs)
    tiled = (AT_RB // SUBLANES, SUBLANES, AT_TQ)
    for t in range(AT_NKB):
        for i, (qk, s_ref) in enumerate(score_jobs):
            s = qk(t)
            for r in range(AT_KT // AT_RB):
                tile = s[r * AT_RB:(r + 1) * AT_RB, :]
                if t == 0:
                    tile = tile + corr_ref[r * AT_RB:(r + 1) * AT_RB, :]
                s_ref[t * AT_KT + r * AT_RB:t * AT_KT + (r + 1) * AT_RB, :] = tile
                part = jnp.max(tile.reshape(tiled), axis=0)
                accs[i] = part if accs[i] is None else jnp.maximum(accs[i], part)
        for i, (s_ref, m, p_ref) in enumerate(softmax_jobs):
            for r in range(AT_KT // AT_RB):
                rows = slice(t * AT_KT + r * AT_RB, t * AT_KT + (r + 1) * AT_RB)
                p = jnp.exp2(s_ref[rows, :].reshape(tiled) - m[None])
                sums[i] = sums[i] + jnp.sum(p, axis=0)
                p_ref[rows, :] = p.reshape(AT_RB, AT_TQ).astype(BF16)
    maxes = [jnp.broadcast_to(jnp.max(a, axis=0, keepdims=True), (SUBLANES, AT_TQ)) for a in accs]
    return maxes, [jnp.sum(x, axis=0, keepdims=True) for x in sums]


def _attn_kernel(lam_init, lam_ref, slope_ref, kc_ref, qc_ref, q_ref, k_ref, v_ref, g_ref, o_ref,
                 kaug_scr, qbase_scr, corr_scr, klr0_scr, klr1_scr, vt_scr, *per_block):
    klr_scr = (klr0_scr, klr1_scr)
    blk_scr = [per_block[AT_BLK_SCR * (n % AT_NBUF):AT_BLK_SCR * (n % AT_NBUF + 1)]
               for n in range(AT_NB)]
    b = pl.program_id(1)
    step = pl.program_id(2)
    lane1 = lax.broadcasted_iota(jnp.int32, (1, HEAD_W), 1) & (DH - 1)

    @pl.when((b == 0) & (step == 0))
    def _():
        is_jl = ((lane1 >= 6) & (lane1 < 9)).astype(F32)
        is_jh = ((lane1 >= 9) & (lane1 < 12)).astype(F32)
        is_il = (lane1 < 3).astype(F32)
        row = lax.broadcasted_iota(jnp.int32, (SEQ, HEAD_W), 0)
        jl = (row & (AT_TQ - 1)).astype(F32)
        jh = (row - (row & (AT_TQ - 1))).astype(F32)
        kaug_scr[...] = kc_ref[0] + is_jl * jl + is_jh * jh
        il = lax.broadcasted_iota(jnp.int32, (AT_TQ, HEAD_W), 0).astype(F32)
        qbase_scr[...] = qc_ref[0] + is_il * il
        kk = lax.broadcasted_iota(jnp.int32, (AT_TQ, AT_TQ), 0)
        qq = lax.broadcasted_iota(jnp.int32, (AT_TQ, AT_TQ), 1)
        corr_scr[...] = -2.0 * slope_ref[0] * jnp.maximum(kk - qq, 0).astype(F32)

    @pl.when(step == 0)
    def _():
        half_k = lax.broadcasted_iota(jnp.int32, (SEQ, HEAD_W), 1) >> 6
        kf = k_ref[0].astype(F32)
        kaug = kaug_scr[...]
        for mp in range(2):
            klr_scr[mp][0] = jnp.where(half_k == mp, kf, kaug).astype(BF16)
            klr_scr[mp][1] = jnp.where(half_k == mp, kf, -kaug).astype(BF16)
        for j in range(AT_NKB):
            vt_scr[j] = v_ref[0, j * AT_KT:(j + 1) * AT_KT, :].astype(F32).T.astype(BF16)

    lam_v = lam_ref[...]
    dots = jnp.sum(lam_v[0:2, :] * lam_v[2:4, :], axis=-1, keepdims=True)
    e = jnp.exp(dots)
    lam = e[0:1, :] - e[1:2, :] + lam_init

    is_q0 = ((lane1 >= 3) & (lane1 < 6)).astype(F32)
    half_q = lax.broadcasted_iota(jnp.int32, (AT_TQ, HEAD_W), 1) >> 6

    def qk_stage(n):
        vtr_scr, s_scr = blk_scr[n][0], blk_scr[n][1:3]
        qi = step * AT_NB + n
        kblk = [(qi + t) & (AT_NKB - 1) for t in range(AT_NKB)]
        for t in range(AT_NKB):
            vtr_scr[:, t * AT_KT:(t + 1) * AT_KT] = vt_scr[kblk[t]]
        qf = q_ref[0, n * AT_TQ:(n + 1) * AT_TQ, :].astype(F32)
        qaug = qbase_scr[...] + is_q0 * (qi * AT_TQ).astype(F32)
        jobs = []
        for mp in range(2):
            qat = jnp.where(half_q == mp, qf, qaug).T.astype(BF16)

            def qk(t, mp=mp, qat=qat):
                side = 0 if t == 0 else 1 - ((qi + t) >> AT_NKB_LOG2)
                src = pl.ds(pl.multiple_of(kblk[t] * AT_KT, AT_KT), AT_KT)
                return jnp.dot(klr_scr[mp][side, src, :], qat, preferred_element_type=F32)

            jobs.append((qk, s_scr[mp]))
        return jobs

    def softmax_jobs(n, mx):
        s_scr, p_scr = blk_scr[n][1:3], blk_scr[n][3:5]
        return [(s_scr[mp], mx[mp], p_scr[mp]) for mp in range(2)]

    def pv_stage(n, denom):
        vtr_scr, p_scr = blk_scr[n][0], blk_scr[n][3:5]
        outs = [jnp.dot(vtr_scr[...], p_scr[mp][...], preferred_element_type=F32) / denom[mp]
                for mp in range(2)]
        ot = outs[0] - lam * outs[1]
        ms = jnp.mean(ot * ot, axis=0, keepdims=True)
        ot = ot * lax.rsqrt(ms + EPS)
        o_ref[0, n * AT_TQ:(n + 1) * AT_TQ, :] = (ot.T * g_ref[...]).astype(BF16)

    mx, _ = _score_softmax_pass(qk_stage(0), [], corr_scr)
    for n in range(AT_NB):
        nxt = qk_stage(n + 1) if n + 1 < AT_NB else []
        mx, denom = _score_softmax_pass(nxt, softmax_jobs(n, mx), corr_scr)
        pv_stage(n, denom)


def _attention(z3, lam_vecs, slopes, kconst, qconst, gain, lam_init):
    qc, kc, vc = Z_Q // HEAD_W, Z_K // HEAD_W, Z_V // HEAD_W
    return pl.pallas_call(
        functools.partial(_attn_kernel, lam_init),
        grid=(HEADS, BATCH, SEQ // (AT_TQ * AT_NB)),
        in_specs=[
            _resident((4, DH)),
            pl.BlockSpec((1, 1, AT_TQ), lambda h, b, i: (h, 0, 0)),
            pl.BlockSpec((1, 1, HEAD_W), lambda h, b, i: (h, 0, 0)),
            pl.BlockSpec((1, 1, HEAD_W), lambda h, b, i: (h, 0, 0)),
            pl.BlockSpec((1, AT_TQ * AT_NB, HEAD_W), lambda h, b, i: (b, i, qc + h)),
            pl.BlockSpec((1, SEQ, HEAD_W), lambda h, b, i: (b, 0, kc + h)),
            pl.BlockSpec((1, SEQ, HEAD_W), lambda h, b, i: (b, 0, vc + h)),
            _resident((1, HEAD_W)),
        ],
        out_specs=pl.BlockSpec((1, AT_TQ * AT_NB, HEAD_W), lambda h, b, i: (b, i, h)),
        out_shape=jax.ShapeDtypeStruct((BATCH, SEQ, HEADS * HEAD_W), BF16),
        scratch_shapes=[
            pltpu.VMEM((SEQ, HEAD_W), F32),
            pltpu.VMEM((AT_TQ, HEAD_W), F32),
            pltpu.VMEM((AT_TQ, AT_TQ), F32),
            pltpu.VMEM((2, SEQ, HEAD_W), BF16),
            pltpu.VMEM((2, SEQ, HEAD_W), BF16),
            pltpu.VMEM((AT_NKB, HEAD_W, AT_KT), BF16),
        ] + AT_NBUF * [
            pltpu.VMEM((HEAD_W, SEQ), BF16),
            pltpu.VMEM((SEQ, AT_TQ), F32),
            pltpu.VMEM((SEQ, AT_TQ), F32),
            pltpu.VMEM((SEQ, AT_TQ), BF16),
            pltpu.VMEM((SEQ, AT_TQ), BF16),
        ],
        compiler_params=_params(("arbitrary", "arbitrary", "arbitrary")),
        name="diff_attention",
    )(lam_vecs, slopes, kconst, qconst, z3, z3, z3, gain)


SG_TS = 512
SG_PC = 256


def _sgu_kernel(su_ref, sv_ref, gl_ref, bg_ref, lng_ref, lnb_ref, ws_ref, bs_ref, wp_ref,
                out_ref, u_scr, gv_scr, sg_scr):
    for n in range(SG_TS // CHUNK):
        rows = slice(n * CHUNK, (n + 1) * CHUNK)
        v = _gelu(sv_ref[rows, :].astype(F32))
        mu = jnp.mean(v, axis=-1, keepdims=True)
        vc = v - mu
        var = jnp.mean(vc * vc, axis=-1, keepdims=True)
        gv_scr[rows, :] = (vc * lax.rsqrt(var + EPS) * lng_ref[...] + lnb_ref[...]).astype(BF16)
        u_scr[rows, :] = _gelu(su_ref[rows, :].astype(F32))
        for g in range(SGU_GROUPS):
            cols = slice(g * LANES, (g + 1) * LANES)
            mixed = jnp.dot(ws_ref[g], gv_scr[rows, cols], preferred_element_type=F32) + bs_ref[g]
            sg_scr[rows, cols] = (u_scr[rows, cols] * mixed).astype(BF16)

    for r in range(SG_TS // SG_PC):
        rows = slice(r * SG_PC, (r + 1) * SG_PC)
        y = jnp.dot(sg_scr[rows, :], wp_ref[...], preferred_element_type=F32)
        gate = _sigmoid(gl_ref[rows, :].astype(F32) + bg_ref[...])
        out_ref[rows, :] = (gate * y).astype(BF16)


def _sgu_branch(z2, bg, lng, lnb, ws, bs, wp, layer):
    def col(c):
        return pl.BlockSpec((SG_TS, D_MODEL), lambda i: (i, c))

    return pl.pallas_call(
        _sgu_kernel,
        grid=(TOKENS // SG_TS,),
        in_specs=[
            col(Z_SU // D_MODEL), col(Z_SV // D_MODEL), col(Z_GC // D_MODEL),
            _resident((1, D_MODEL)),
            _resident((1, D_MODEL)),
            _resident((1, D_MODEL)),
            _layer_weight((SGU_GROUPS, CHUNK, CHUNK), layer),
            _resident((SGU_GROUPS, CHUNK, LANES)),
            _layer_weight((D_MODEL, D_MODEL), layer),
        ],
        out_specs=pl.BlockSpec((SG_TS, D_MODEL), lambda i: (i, 0)),
        out_shape=jax.ShapeDtypeStruct((TOKENS, D_MODEL), BF16),
        scratch_shapes=[
            pltpu.VMEM((SG_TS, D_MODEL), F32),
            pltpu.VMEM((SG_TS, D_MODEL), BF16),
            pltpu.VMEM((SG_TS, D_MODEL), BF16),
        ],
        compiler_params=_params(("parallel",)),
        name="sgu_branch",
    )(z2, z2, z2, bg, lng, lnb, ws, bs, wp)


MG_TM = 1024
MG_RC = 256


def _merge_kernel(x_ref, ya_ref, o_ref, yc_ref, gl_ref, bg_ref, wa_ref, wout_ref, gpost_ref, out_ref):
    for r in range(MG_TM // MG_RC):
        rows = slice(r * MG_RC, (r + 1) * MG_RC)
        yb = jnp.dot(o_ref[rows, :], wa_ref[...], preferred_element_type=F32)
        gate = _sigmoid(gl_ref[rows, :].astype(F32) + bg_ref[...])
        mix = ya_ref[rows, :].astype(F32) + gate * yb + yc_ref[rows, :].astype(F32)
        m = jnp.dot(mix.astype(BF16), wout_ref[...], preferred_element_type=F32)
        out_ref[rows, :] = x_ref[rows, :] + _rms(m, gpost_ref[...])


def _merge(x2, ya, o, yc, z2, bg, wa, wout, gpost, layer):
    def rows(width=D_MODEL, col=0):
        return pl.BlockSpec((MG_TM, width), lambda i: (i, col))

    return pl.pallas_call(
        _merge_kernel,
        grid=(TOKENS // MG_TM,),
        in_specs=[
            rows(), rows(), rows(), rows(), rows(col=Z_GB // D_MODEL),
            _resident((1, D_MODEL)),
            _layer_weight((D_MODEL, D_MODEL), layer),
            _layer_weight((D_MODEL, D_MODEL), layer),
            _resident((1, D_MODEL)),
        ],
        out_specs=rows(),
        out_shape=jax.ShapeDtypeStruct((TOKENS, D_MODEL), F32),
        compiler_params=_params(("parallel",)),
        name="merge",
    )(x2, ya, o, yc, z2, bg, wa, wout, gpost)


FF_TM = 1024
FF_RC = 512
FF_TC = 1024


def _ffn_kernel(x_ref, gpre_ref, gpost_ref, wup_ref, wdn_ref, out_ref):
    for r in range(FF_TM // FF_RC):
        rows = slice(r * FF_RC, (r + 1) * FF_RC)
        x = x_ref[rows, :]
        h = _rms(x, gpre_ref[...]).astype(BF16)
        acc = jnp.zeros((FF_RC, D_MODEL), F32)
        for c in range(D_FF // FF_TC):
            cols = slice(c * FF_TC, (c + 1) * FF_TC)
            u = jnp.dot(h, wup_ref[:, cols], preferred_element_type=F32)
            u = jnp.square(jnp.maximum(u, 0.0)).astype(BF16)
            acc = acc + jnp.dot(u, wdn_ref[cols, :], preferred_element_type=F32)
        out_ref[rows, :] = x + _rms(acc, gpost_ref[...])


def _ffn(x2, gpre, gpost, wup, wdn, layer):
    return pl.pallas_call(
        _ffn_kernel,
        grid=(TOKENS // FF_TM,),
        in_specs=[
            pl.BlockSpec((FF_TM, D_MODEL), lambda i: (i, 0)),
            _resident((1, D_MODEL)),
            _resident((1, D_MODEL)),
            _layer_weight((D_MODEL, D_FF), layer),
            _layer_weight((D_FF, D_MODEL), layer),
        ],
        out_specs=pl.BlockSpec((FF_TM, D_MODEL), lambda i: (i, 0)),
        out_shape=jax.ShapeDtypeStruct((TOKENS, D_MODEL), F32),
        compiler_params=_params(("parallel",)),
        name="ffn",
    )(x2, gpre, gpost, wup, wdn)


def _row(v):
    return v.reshape(1, -1).astype(F32)


def kernel(x, norm_mix_pre, norm_mix_post, w_in, b_gate, conv_w, conv_b, conv_ln_g, conv_ln_b,
           lam_q1, lam_k1, lam_q2, lam_k2, subln_g, sgu_ln_g, sgu_ln_b, sgu_w, sgu_b,
           w_proj_conv, w_proj_attn, w_proj_sgu, w_out, norm_ffn_pre, norm_ffn_post,
           w_ffn_up, w_ffn_down):
    assert x.shape == (BATCH, SEQ, D_MODEL) and w_in.shape == (DEPTH, D_MODEL, Z_COLS)
    x2 = x.reshape(TOKENS, D_MODEL).astype(F32)

    slopes = jnp.exp2(-8.0 * jnp.arange(1, HEADS + 1, dtype=F32) / HEADS) * LOG2E
    c1 = slopes.astype(BF16).astype(F32)
    c2 = (slopes - c1).astype(BF16).astype(F32)
    c3 = (slopes - c1 - c2).astype(BF16).astype(F32)
    pieces = jnp.stack([c1, c2, c3], axis=1)
    a = jnp.arange(HEAD_W) & (DH - 1)
    kconst = jnp.where(a < 6, -pieces[:, a % 3], 0.0)[:, None, :]
    qconst = jnp.where((a >= 6) & (a < 12), pieces[:, a % 3], 0.0)[:, None, :]
    slopes = jnp.broadcast_to(slopes[:, None, None], (HEADS, 1, AT_TQ))

    w_in_b, w_conv_b, w_attn_b, w_sgu_b, w_out_b, sgu_w_b, w_up_b, w_dn_b = (
        w.astype(BF16) for w in (w_in, w_proj_conv, w_proj_attn, w_proj_sgu, w_out, sgu_w,
                                 w_ffn_up, w_ffn_down))

    for l in range(DEPTH):
        lam_init = 0.8 - 0.6 * math.exp(-0.3 * l)
        cw = conv_w[l].reshape(CONV_K, CV_NC, LANES).transpose(1, 0, 2).astype(F32)
        cb = conv_b[l].reshape(CV_NC, 1, LANES).astype(F32)
        z2, conv_raw = _proj_in(x2, _row(norm_mix_pre[l]), w_in_b, l, cw, cb)
        z3 = z2.reshape(BATCH, SEQ, Z_COLS)

        ya = _conv_branch(conv_raw, z2, _row(b_gate[l, 0:D_MODEL]),
                          _row(conv_ln_g[l]), _row(conv_ln_b[l]), w_conv_b, l)

        lam_vecs = jnp.stack([lam_q1[l], lam_q2[l], lam_k1[l], lam_k2[l]]).astype(F32)
        o = _attention(z3, lam_vecs, slopes, kconst, qconst,
                       _row(subln_g[l]) * (1.0 - lam_init), lam_init)

        bs = jnp.broadcast_to(sgu_b[l].astype(F32)[:, :, None], (SGU_GROUPS, CHUNK, LANES))
        yc = _sgu_branch(z2, _row(b_gate[l, 2 * D_MODEL:3 * D_MODEL]), _row(sgu_ln_g[l]),
                         _row(sgu_ln_b[l]), sgu_w_b, bs, w_sgu_b, l)

        x2 = _merge(x2, ya, o.reshape(TOKENS, D_MODEL), yc, z2,
                    _row(b_gate[l, D_MODEL:2 * D_MODEL]), w_attn_b, w_out_b,
                    _row(norm_mix_post[l]), l)

        x2 = _ffn(x2, _row(norm_ffn_pre[l]), _row(norm_ffn_post[l]), w_up_b, w_dn_b, l)

    return x2.reshape(BATCH, SEQ, D_MODEL).astype(x.dtype)
```

```python
import functools
import math

import jax
import jax.numpy as jnp
from jax import lax
from jax.experimental import pallas as pl
from jax.experimental.pallas import tpu as pltpu

F32 = jnp.float32
BF16 = jnp.bfloat16

D_MODEL = 1024
BATCH = 8
SEQ = 2048
DEPTH = 2
TOKENS = BATCH * SEQ
EPS = 1e-6

CONV_K = 31
CONV_PAD = (CONV_K - 1) // 2
HEADS = 8
DH = 64
HEAD_W = 2 * DH
SGU_GROUPS = 8
CHUNK = 128
D_FF = 4 * D_MODEL

Z_A, Z_B, Z_Q, Z_K, Z_V, Z_SU, Z_SV, Z_GA, Z_GB, Z_GC = (i * D_MODEL for i in range(10))
Z_COLS = 10 * D_MODEL

LANES = 128
SUBLANES = 8
HALO = 16
LOG2E = 1.4426950408889634

VMEM_LIMIT = 56 * 1024 * 1024


def _params(semantics):
    return pltpu.CompilerParams(dimension_semantics=semantics, vmem_limit_bytes=VMEM_LIMIT)


def _resident(shape):
    zeros = (0,) * len(shape)
    return pl.BlockSpec(shape, lambda *_: zeros, pipeline_mode=pl.Buffered(1))


def _layer_weight(shape, layer):
    zeros = (0,) * len(shape)
    return pl.BlockSpec((None,) + shape, lambda *_: (layer,) + zeros, pipeline_mode=pl.Buffered(1))


def _rms(x, g):
    ms = jnp.mean(x * x, axis=-1, keepdims=True)
    return x * lax.rsqrt(ms + EPS) * g


def _sigmoid(x):
    return 0.5 * jnp.tanh(0.5 * x) + 0.5


def _gelu(x):
    return 0.5 * x * (1.0 + lax.erf(x * (2.0 ** -0.5)))


PI_TM = SEQ
PI_TN = 1024
PI_RC = 256
PI_CONV_J0 = 2
CV_R = 128
CV_NC = D_MODEL // LANES
assert Z_A == 0 and Z_B == PI_TN and Z_COLS // PI_TN == PI_CONV_J0 + CV_NC


def _proj_in_kernel(x_ref, g_ref, cs_ref, w_ref, cw_ref, cb_ref, z_ref, cv_ref, h_ref, zab_scr, a_scr):
    j = pl.program_id(1)

    @pl.when(j == 0)
    def _():
        for r in range(PI_TM // PI_RC):
            rows = slice(r * PI_RC, (r + 1) * PI_RC)
            h_ref[rows, :] = _rms(x_ref[rows, :], g_ref[...]).astype(BF16)

    def project(r, keep):
        rows = slice(r * PI_RC, (r + 1) * PI_RC)
        acc = jnp.dot(h_ref[rows, :], w_ref[...], preferred_element_type=F32)
        zb = (acc * cs_ref[...]).astype(BF16)
        z_ref[rows, :] = zb
        if keep:
            for c in range(CV_NC):
                zab_scr[j, c, rows, :] = zb[:, c * LANES:(c + 1) * LANES]

    @pl.when(j < PI_CONV_J0)
    def _():
        for r in range(PI_TM // PI_RC):
            project(r, keep=True)

    @pl.when(j >= PI_CONV_J0)
    def _():
        c = j - PI_CONV_J0
        zero = jnp.zeros((HALO, LANES), F32)
        a_scr[0:HALO, :] = zero
        a_scr[HALO + SEQ:, :] = zero
        for r in range(SEQ // PI_RC):
            rows = slice(r * PI_RC, (r + 1) * PI_RC)
            glu = zab_scr[0, c, rows, :].astype(F32) * _sigmoid(zab_scr[1, c, rows, :].astype(F32))
            a_scr[HALO + r * PI_RC:HALO + (r + 1) * PI_RC, :] = glu
        for r in range(PI_TM // PI_RC):
            project(r, keep=False)
            for rr in range(r * (PI_RC // CV_R), (r + 1) * (PI_RC // CV_R)):
                acc = jnp.zeros((CV_R, LANES), F32)
                for t in range(CONV_K):
                    start = rr * CV_R + HALO - CONV_PAD + t
                    acc = acc + cw_ref[0, t:t + 1, :] * a_scr[start:start + CV_R, :]
                cv_ref[rr * CV_R:(rr + 1) * CV_R, :] = acc + cb_ref[0]


def _proj_in(x2, g, colscale, w, layer, cw, cb):
    def chunk(i, j):
        return jnp.maximum(j - PI_CONV_J0, 0)

    return pl.pallas_call(
        _proj_in_kernel,
        grid=(TOKENS // PI_TM, Z_COLS // PI_TN),
        in_specs=[
            pl.BlockSpec((PI_TM, D_MODEL), lambda i, j: (i, 0)),
            _resident((1, D_MODEL)),
            pl.BlockSpec((1, PI_TN), lambda i, j: (0, j)),
            pl.BlockSpec((None, D_MODEL, PI_TN), lambda i, j: (layer, 0, j)),
            pl.BlockSpec((1, CONV_K, LANES), lambda i, j: (chunk(i, j), 0, 0)),
            pl.BlockSpec((1, 1, LANES), lambda i, j: (chunk(i, j), 0, 0)),
        ],
        out_specs=[
            pl.BlockSpec((PI_TM, PI_TN), lambda i, j: (i, j)),
            pl.BlockSpec((PI_TM, LANES), lambda i, j: (i, chunk(i, j))),
        ],
        out_shape=[
            jax.ShapeDtypeStruct((TOKENS, Z_COLS), BF16),
            jax.ShapeDtypeStruct((TOKENS, D_MODEL), F32),
        ],
        scratch_shapes=[
            pltpu.VMEM((PI_TM, D_MODEL), BF16),
            pltpu.VMEM((PI_CONV_J0, CV_NC, SEQ, LANES), BF16),
            pltpu.VMEM((SEQ + 2 * HALO, LANES), F32),
        ],
        compiler_params=_params(("parallel", "arbitrary")),
        name="proj_in",
    )(x2, g, colscale, w, cw, cb)


CV_TS = 1024
CV_RC = 128
CV_PC = 256


def _conv_kernel(cv_ref, gl_ref, bg_ref, lng_ref, lnb_ref, wp_ref, out_ref, act_scr):
    for r in range(CV_TS // CV_RC):
        rows = slice(r * CV_RC, (r + 1) * CV_RC)
        c = cv_ref[rows, :]
        mu = jnp.mean(c, axis=-1, keepdims=True)
        d = c - mu
        var = jnp.mean(d * d, axis=-1, keepdims=True)
        y = d * lax.rsqrt(var + EPS) * lng_ref[...] + lnb_ref[...]
        act_scr[rows, :] = (y * _sigmoid(y)).astype(BF16)

    for r in range(CV_TS // CV_PC):
        rows = slice(r * CV_PC, (r + 1) * CV_PC)
        y = jnp.dot(act_scr[rows, :], wp_ref[...], preferred_element_type=F32)
        gate = _sigmoid(gl_ref[rows, :].astype(F32) + bg_ref[...])
        out_ref[rows, :] = (gate * y).astype(BF16)


def _conv_branch(conv_raw, z2, bg, lng, lnb, wp, layer):
    return pl.pallas_call(
        _conv_kernel,
        grid=(TOKENS // CV_TS,),
        in_specs=[
            pl.BlockSpec((CV_TS, D_MODEL), lambda i: (i, 0)),
            pl.BlockSpec((CV_TS, D_MODEL), lambda i: (i, Z_GA // D_MODEL)),
            _resident((1, D_MODEL)),
            _resident((1, D_MODEL)),
            _resident((1, D_MODEL)),
            _layer_weight((D_MODEL, D_MODEL), layer),
        ],
        out_specs=pl.BlockSpec((CV_TS, D_MODEL), lambda i: (i, 0)),
        out_shape=jax.ShapeDtypeStruct((TOKENS, D_MODEL), BF16),
        scratch_shapes=[pltpu.VMEM((CV_TS, D_MODEL), BF16)],
        compiler_params=_params(("parallel",)),
        name="conv_branch",
    )(conv_raw, z2, bg, lng, lnb, wp)


AT_TQ = 256
AT_NB = 8
AT_NBUF = 2
AT_RB = 64
AT_KT = 256
AT_NKB = SEQ // AT_KT
AT_NKB_LOG2 = AT_NKB.bit_length() - 1
AT_BLK_SCR = 5
assert AT_NKB == 1 << AT_NKB_LOG2 and AT_KT == AT_TQ


def _score_softmax_pass(score_jobs, softmax_jobs, corr_ref):
    accs = [None] * len(score_jobs)
    sums = [jnp.zeros((SUBLANES, AT_TQ), F32)] * len(softmax_jobs)
    tiled = (AT_RB // SUBLANES, SUBLANES, AT_TQ)
    for t in range(AT_NKB):
        for i, (qk, s_ref) in enumerate(score_jobs):
            s = qk(t)
            for r in range(AT_KT // AT_RB):
                tile = s[r * AT_RB:(r + 1) * AT_RB, :]
                if t == 0:
                    tile = tile + corr_ref[r * AT_RB:(r + 1) * AT_RB, :]
                s_ref[t * AT_KT + r * AT_RB:t * AT_KT + (r + 1) * AT_RB, :] = tile
                part = jnp.max(tile.reshape(tiled), axis=0)
                accs[i] = part if accs[i] is None else jnp.maximum(accs[i], part)
        for i, (s_ref, m, p_ref) in enumerate(softmax_jobs):
            for r in range(AT_KT // AT_RB):
                rows = slice(t * AT_KT + r * AT_RB, t * AT_KT + (r + 1) * AT_RB)
                p = jnp.exp2(s_ref[rows, :].reshape(tiled) - m[None])
                sums[i] = sums[i] + jnp.sum(p, axis=0)
                p_ref[rows, :] = p.reshape(AT_RB, AT_TQ).astype(BF16)
    maxes = [jnp.broadcast_to(jnp.max(a, axis=0, keepdims=True), (SUBLANES, AT_TQ)) for a in accs]
    return maxes, [jnp.sum(x, axis=0, keepdims=True) for x in sums]


def _attn_kernel(lam_init, lam_ref, slope_ref, kc_ref, qc_ref, q_ref, k_ref, v_ref, g_ref, o_ref,
                 kaug_scr, qbase_scr, corr_scr, klr0_scr, klr1_scr, vt_scr, *per_block):
    klr_scr = (klr0_scr, klr1_scr)
    blk_scr = [per_block[AT_BLK_SCR * (n % AT_NBUF):AT_BLK_SCR * (n % AT_NBUF + 1)]
               for n in range(AT_NB)]
    b = pl.program_id(1)
    step = pl.program_id(2)
    lane1 = lax.broadcasted_iota(jnp.int32, (1, HEAD_W), 1) & (DH - 1)

    @pl.when((b == 0) & (step == 0))
    def _():
        is_jl = ((lane1 >= 6) & (lane1 < 9)).astype(F32)
        is_jh = ((lane1 >= 9) & (lane1 < 12)).astype(F32)
        is_il = (lane1 < 3).astype(F32)
        row = lax.broadcasted_iota(jnp.int32, (SEQ, HEAD_W), 0)
        jl = (row & (AT_TQ - 1)).astype(F32)
        jh = (row - (row & (AT_TQ - 1))).astype(F32)
        kaug_scr[...] = kc_ref[0] + is_jl * jl + is_jh * jh
        il = lax.broadcasted_iota(jnp.int32, (AT_TQ, HEAD_W), 0).astype(F32)
        qbase_scr[...] = qc_ref[0] + is_il * il
        kk = lax.broadcasted_iota(jnp.int32, (AT_TQ, AT_TQ), 0)
        qq = lax.broadcasted_iota(jnp.int32, (AT_TQ, AT_TQ), 1)
        corr_scr[...] = -2.0 * slope_ref[0] * jnp.maximum(kk - qq, 0).astype(F32)

    @pl.when(step == 0)
    def _():
        half_k = lax.broadcasted_iota(jnp.int32, (SEQ, HEAD_W), 1) >> 6
        kf = k_ref[0].astype(F32)
        kaug = kaug_scr[...]
        for mp in range(2):
            klr_scr[mp][0] = jnp.where(half_k == mp, kf, kaug).astype(BF16)
            klr_scr[mp][1] = jnp.where(half_k == mp, kf, -kaug).astype(BF16)
        for j in range(AT_NKB):
            vt_scr[j] = v_ref[0, j * AT_KT:(j + 1) * AT_KT, :].astype(F32).T.astype(BF16)

    lam_v = lam_ref[...]
    dots = jnp.sum(lam_v[0:2, :] * lam_v[2:4, :], axis=-1, keepdims=True)
    e = jnp.exp(dots)
    lam = e[0:1, :] - e[1:2, :] + lam_init

    is_q0 = ((lane1 >= 3) & (lane1 < 6)).astype(F32)
    half_q = lax.broadcasted_iota(jnp.int32, (AT_TQ, HEAD_W), 1) >> 6

    def qk_stage(n):
        vtr_scr, s_scr = blk_scr[n][0], blk_scr[n][1:3]
        qi = step * AT_NB + n
        kblk = [(qi + t) & (AT_NKB - 1) for t in range(AT_NKB)]
        for t in range(AT_NKB):
            vtr_scr[:, t * AT_KT:(t + 1) * AT_KT] = vt_scr[kblk[t]]
        qf = q_ref[0, n * AT_TQ:(n + 1) * AT_TQ, :].astype(F32)
        qaug = qbase_scr[...] + is_q0 * (qi * AT_TQ).astype(F32)
        jobs = []
        for mp in range(2):
            qat = jnp.where(half_q == mp, qf, qaug).T.astype(BF16)

            def qk(t, mp=mp, qat=qat):
                side = 0 if t == 0 else 1 - ((qi + t) >> AT_NKB_LOG2)
                src = pl.ds(pl.multiple_of(kblk[t] * AT_KT, AT_KT), AT_KT)
                return jnp.dot(klr_scr[mp][side, src, :], qat, preferred_element_type=F32)

            jobs.append((qk, s_scr[mp]))
        return jobs

    def softmax_jobs(n, mx):
        s_scr, p_scr = blk_scr[n][1:3], blk_scr[n][3:5]
        return [(s_scr[mp], mx[mp], p_scr[mp]) for mp in range(2)]

    def pv_stage(n, denom):
        vtr_scr, p_scr = blk_scr[n][0], blk_scr[n][3:5]
        outs = [jnp.dot(vtr_scr[...], p_scr[mp][...], preferred_element_type=F32) / denom[mp]
                for mp in range(2)]
        ot = outs[0] - lam * outs[1]
        ms = jnp.mean(ot * ot, axis=0, keepdims=True)
        ot = ot * lax.rsqrt(ms + EPS)
        o_ref[0, n * AT_TQ:(n + 1) * AT_TQ, :] = (ot.T * g_ref[...]).astype(BF16)

    mx, _ = _score_softmax_pass(qk_stage(0), [], corr_scr)
    for n in range(AT_NB):
        nxt = qk_stage(n + 1) if n + 1 < AT_NB else []
        mx, denom = _score_softmax_pass(nxt, softmax_jobs(n, mx), corr_scr)
        pv_stage(n, denom)


def _attention(z3, lam_vecs, slopes, kconst, qconst, gain, lam_init):
    qc, kc, vc = Z_Q // HEAD_W, Z_K // HEAD_W, Z_V // HEAD_W
    return pl.pallas_call(
        functools.partial(_attn_kernel, lam_init),
        grid=(HEADS, BATCH, SEQ // (AT_TQ * AT_NB)),
        in_specs=[
            _resident((4, DH)),
            pl.BlockSpec((1, 1, AT_TQ), lambda h, b, i: (h, 0, 0)),
            pl.BlockSpec((1, 1, HEAD_W), lambda h, b, i: (h, 0, 0)),
            pl.BlockSpec((1, 1, HEAD_W), lambda h, b, i: (h, 0, 0)),
            pl.BlockSpec((1, AT_TQ * AT_NB, HEAD_W), lambda h, b, i: (b, i, qc + h)),
            pl.BlockSpec((1, SEQ, HEAD_W), lambda h, b, i: (b, 0, kc + h)),
            pl.BlockSpec((1, SEQ, HEAD_W), lambda h, b, i: (b, 0, vc + h)),
            _resident((1, HEAD_W)),
        ],
        out_specs=pl.BlockSpec((1, AT_TQ * AT_NB, HEAD_W), lambda h, b, i: (b, i, h)),
        out_shape=jax.ShapeDtypeStruct((BATCH, SEQ, HEADS * HEAD_W), BF16),
        scratch_shapes=[
            pltpu.VMEM((SEQ, HEAD_W), F32),
            pltpu.VMEM((AT_TQ, HEAD_W), F32),
            pltpu.VMEM((AT_TQ, AT_TQ), F32),
            pltpu.VMEM((2, SEQ, HEAD_W), BF16),
            pltpu.VMEM((2, SEQ, HEAD_W), BF16),
            pltpu.VMEM((AT_NKB, HEAD_W, AT_KT), BF16),
        ] + AT_NBUF * [
            pltpu.VMEM((HEAD_W, SEQ), BF16),
            pltpu.VMEM((SEQ, AT_TQ), F32),
            pltpu.VMEM((SEQ, AT_TQ), F32),
            pltpu.VMEM((SEQ, AT_TQ), BF16),
            pltpu.VMEM((SEQ, AT_TQ), BF16),
        ],
        compiler_params=_params(("arbitrary", "arbitrary", "arbitrary")),
        name="diff_attention",
    )(lam_vecs, slopes, kconst, qconst, z3, z3, z3, gain)


SG_TS = 512
SG_PC = 256


def _sgu_kernel(su_ref, sv_ref, gl_ref, bg_ref, lng_ref, lnb_ref, ws_ref, bs_ref, wp_ref,
                out_ref, u_scr, gv_scr, sg_scr):
    for n in range(SG_TS // CHUNK):
        rows = slice(n * CHUNK, (n + 1) * CHUNK)
        v = _gelu(sv_ref[rows, :].astype(F32))
        mu = jnp.mean(v, axis=-1, keepdims=True)
        vc = v - mu
        var = jnp.mean(vc * vc, axis=-1, keepdims=True)
        gv_scr[rows, :] = (vc * lax.rsqrt(var + EPS) * lng_ref[...] + lnb_ref[...]).astype(BF16)
        u_scr[rows, :] = _gelu(su_ref[rows, :].astype(F32))
        for g in range(SGU_GROUPS):
            cols = slice(g * LANES, (g + 1) * LANES)
            mixed = jnp.dot(ws_ref[g], gv_scr[rows, cols], preferred_element_type=F32) + bs_ref[g]
            sg_scr[rows, cols] = (u_scr[rows, cols] * mixed).astype(BF16)

    for r in range(SG_TS // SG_PC):
        rows = slice(r * SG_PC, (r + 1) * SG_PC)
        y = jnp.dot(sg_scr[rows, :], wp_ref[...], preferred_element_type=F32)
        gate = _sigmoid(gl_ref[rows, :].astype(F32) + bg_ref[...])
        out_ref[rows, :] = (gate * y).astype(BF16)


def _sgu_branch(z2, bg, lng, lnb, ws, bs, wp, layer):
    def col(c):
        return pl.BlockSpec((SG_TS, D_MODEL), lambda i: (i, c))

    return pl.pallas_call(
        _sgu_kernel,
        grid=(TOKENS // SG_TS,),
        in_specs=[
            col(Z_SU // D_MODEL), col(Z_SV // D_MODEL), col(Z_GC // D_MODEL),
            _resident((1, D_MODEL)),
            _resident((1, D_MODEL)),
            _resident((1, D_MODEL)),
            _layer_weight((SGU_GROUPS, CHUNK, CHUNK), layer),
            _resident((SGU_GROUPS, CHUNK, LANES)),
            _layer_weight((D_MODEL, D_MODEL), layer),
        ],
        out_specs=pl.BlockSpec((SG_TS, D_MODEL), lambda i: (i, 0)),
        out_shape=jax.ShapeDtypeStruct((TOKENS, D_MODEL), BF16),
        scratch_shapes=[
            pltpu.VMEM((SG_TS, D_MODEL), F32),
            pltpu.VMEM((SG_TS, D_MODEL), BF16),
            pltpu.VMEM((SG_TS, D_MODEL), BF16),
        ],
        compiler_params=_params(("parallel",)),
        name="sgu_branch",
    )(z2, z2, z2, bg, lng, lnb, ws, bs, wp)


MG_TM = 1024
MG_RC = 256


def _merge_kernel(x_ref, ya_ref, o_ref, yc_ref, gl_ref, bg_ref, wa_ref, wout_ref, gpost_ref, out_ref):
    for r in range(MG_TM // MG_RC):
        rows = slice(r * MG_RC, (r + 1) * MG_RC)
        yb = jnp.dot(o_ref[rows, :], wa_ref[...], preferred_element_type=F32)
        gate = _sigmoid(gl_ref[rows, :].astype(F32) + bg_ref[...])
        mix = ya_ref[rows, :].astype(F32) + gate * yb + yc_ref[rows, :].astype(F32)
        m = jnp.dot(mix.astype(BF16), wout_ref[...], preferred_element_type=F32)
        out_ref[rows, :] = x_ref[rows, :] + _rms(m, gpost_ref[...])


def _merge(x2, ya, o, yc, z2, bg, wa, wout, gpost, layer):
    def rows(width=D_MODEL, col=0):
        return pl.BlockSpec((MG_TM, width), lambda i: (i, col))

    return pl.pallas_call(
        _merge_kernel,
        grid=(TOKENS // MG_TM,),
        in_specs=[
            rows(), rows(), rows(), rows(), rows(col=Z_GB // D_MODEL),
            _resident((1, D_MODEL)),
            _layer_weight((D_MODEL, D_MODEL), layer),
            _layer_weight((D_MODEL, D_MODEL), layer),
            _resident((1, D_MODEL)),
        ],
        out_specs=rows(),
        out_shape=jax.ShapeDtypeStruct((TOKENS, D_MODEL), F32),
        compiler_params=_params(("parallel",)),
        name="merge",
    )(x2, ya, o, yc, z2, bg, wa, wout, gpost)


FF_TM = 1024
FF_RC = 512
FF_TC = 1024


def _ffn_kernel(x_ref, gpre_ref, gpost_ref, wup_ref, wdn_ref, out_ref):
    for r in range(FF_TM // FF_RC):
        rows = slice(r * FF_RC, (r + 1) * FF_RC)
        x = x_ref[rows, :]
        h = _rms(x, gpre_ref[...]).astype(BF16)
        acc = jnp.zeros((FF_RC, D_MODEL), F32)
        for c in range(D_FF // FF_TC):
            cols = slice(c * FF_TC, (c + 1) * FF_TC)
            u = jnp.dot(h, wup_ref[:, cols], preferred_element_type=F32)
            u = jnp.square(jnp.maximum(u, 0.0)).astype(BF16)
            acc = acc + jnp.dot(u, wdn_ref[cols, :], preferred_element_type=F32)
        out_ref[rows, :] = x + _rms(acc, gpost_ref[...])


def _ffn(x2, gpre, gpost, wup, wdn, layer):
    return pl.pallas_call(
        _ffn_kernel,
        grid=(TOKENS // FF_TM,),
        in_specs=[
            pl.BlockSpec((FF_TM, D_MODEL), lambda i: (i, 0)),
            _resident((1, D_MODEL)),
            _resident((1, D_MODEL)),
            _layer_weight((D_MODEL, D_FF), layer),
            _layer_weight((D_FF, D_MODEL), layer),
        ],
        out_specs=pl.BlockSpec((FF_TM, D_MODEL), lambda i: (i, 0)),
        out_shape=jax.ShapeDtypeStruct((TOKENS, D_MODEL), F32),
        compiler_params=_params(("parallel",)),
        name="ffn",
    )(x2, gpre, gpost, wup, wdn)


def _row(v):
    return v.reshape(1, -1).astype(F32)


def kernel(x, norm_mix_pre, norm_mix_post, w_in, b_gate, conv_w, conv_b, conv_ln_g, conv_ln_b,
           lam_q1, lam_k1, lam_q2, lam_k2, subln_g, sgu_ln_g, sgu_ln_b, sgu_w, sgu_b,
           w_proj_conv, w_proj_attn, w_proj_sgu, w_out, norm_ffn_pre, norm_ffn_post,
           w_ffn_up, w_ffn_down):
    assert x.shape == (BATCH, SEQ, D_MODEL) and w_in.shape == (DEPTH, D_MODEL, Z_COLS)
    x2 = x.reshape(TOKENS, D_MODEL).astype(F32)

    colscale = jnp.ones((1, Z_COLS), F32).at[:, Z_Q:Z_Q + D_MODEL].set(DH ** -0.5 * LOG2E)
    slopes = jnp.exp2(-8.0 * jnp.arange(1, HEADS + 1, dtype=F32) / HEADS) * LOG2E
    c1 = slopes.astype(BF16).astype(F32)
    c2 = (slopes - c1).astype(BF16).astype(F32)
    c3 = (slopes - c1 - c2).astype(BF16).astype(F32)
    pieces = jnp.stack([c1, c2, c3], axis=1)
    a = jnp.arange(HEAD_W) & (DH - 1)
    kconst = jnp.where(a < 6, -pieces[:, a % 3], 0.0)[:, None, :]
    qconst = jnp.where((a >= 6) & (a < 12), pieces[:, a % 3], 0.0)[:, None, :]
    slopes = jnp.broadcast_to(slopes[:, None, None], (HEADS, 1, AT_TQ))

    w_in_b, w_conv_b, w_attn_b, w_sgu_b, w_out_b, sgu_w_b, w_up_b, w_dn_b = (
        w.astype(BF16) for w in (w_in, w_proj_conv, w_proj_attn, w_proj_sgu, w_out, sgu_w,
                                 w_ffn_up, w_ffn_down))

    for l in range(DEPTH):
        lam_init = 0.8 - 0.6 * math.exp(-0.3 * l)
        cw = conv_w[l].reshape(CONV_K, CV_NC, LANES).transpose(1, 0, 2).astype(F32)
        cb = conv_b[l].reshape(CV_NC, 1, LANES).astype(F32)
        z2, conv_raw = _proj_in(x2, _row(norm_mix_pre[l]), colscale, w_in_b, l, cw, cb)
        z3 = z2.reshape(BATCH, SEQ, Z_COLS)

        ya = _conv_branch(conv_raw, z2, _row(b_gate[l, 0:D_MODEL]),
                          _row(conv_ln_g[l]), _row(conv_ln_b[l]), w_conv_b, l)

        lam_vecs = jnp.stack([lam_q1[l], lam_q2[l], lam_k1[l], lam_k2[l]]).astype(F32)
        o = _attention(z3, lam_vecs, slopes, kconst, qconst,
                       _row(subln_g[l]) * (1.0 - lam_init), lam_init)

        bs = jnp.broadcast_to(sgu_b[l].astype(F32)[:, :, None], (SGU_GROUPS, CHUNK, LANES))
        yc = _sgu_branch(z2, _row(b_gate[l, 2 * D_MODEL:3 * D_MODEL]), _row(sgu_ln_g[l]),
                         _row(sgu_ln_b[l]), sgu_w_b, bs, w_sgu_b, l)

        x2 = _merge(x2, ya, o.reshape(TOKENS, D_MODEL), yc, z2,
                    _row(b_gate[l, D_MODEL:2 * D_MODEL]), w_attn_b, w_out_b,
                    _row(norm_mix_post[l]), l)

        x2 = _ffn(x2, _row(norm_ffn_pre[l]), _row(norm_ffn_post[l]), w_up_b, w_dn_b, l)

    return x2.reshape(BATCH, SEQ, D_MODEL).astype(x.dtype)
```

```python
import functools
import math

import jax
import jax.numpy as jnp
from jax import lax
from jax.experimental import pallas as pl
from jax.experimental.pallas import tpu as pltpu

F32 = jnp.float32
BF16 = jnp.bfloat16

D_MODEL = 1024
BATCH = 8
SEQ = 2048
DEPTH = 2
TOKENS = BATCH * SEQ
EPS = 1e-6

CONV_K = 31
CONV_PAD = (CONV_K - 1) // 2
HEADS = 8
DH = 64
HEAD_W = 2 * DH
SGU_GROUPS = 8
CHUNK = 128
D_FF = 4 * D_MODEL

Z_A, Z_B, Z_Q, Z_K, Z_V, Z_SU, Z_SV, Z_GA, Z_GB, Z_GC = (i * D_MODEL for i in range(10))
Z_COLS = 10 * D_MODEL

LANES = 128
SUBLANES = 8
HALO = 16
LOG2E = 1.4426950408889634

VMEM_LIMIT = 56 * 1024 * 1024


def _params(semantics):
    return pltpu.CompilerParams(dimension_semantics=semantics, vmem_limit_bytes=VMEM_LIMIT)


def _resident(shape):
    zeros = (0,) * len(shape)
    return pl.BlockSpec(shape, lambda *_: zeros, pipeline_mode=pl.Buffered(1))


def _layer_weight(shape, layer):
    zeros = (0,) * len(shape)
    return pl.BlockSpec((None,) + shape, lambda *_: (layer,) + zeros, pipeline_mode=pl.Buffered(1))


def _rms(x, g):
    ms = jnp.mean(x * x, axis=-1, keepdims=True)
    return x * lax.rsqrt(ms + EPS) * g


def _sigmoid(x):
    return 0.5 * jnp.tanh(0.5 * x) + 0.5


def _gelu(x):
    return 0.5 * x * (1.0 + lax.erf(x * (2.0 ** -0.5)))


PI_TM = SEQ
PI_TN = 2048
PI_RC = 256
PI_CONV_J0 = 1
PI_CPS = 2
CV_R = 128
CV_NC = D_MODEL // LANES
assert Z_A == 0 and Z_B == D_MODEL and PI_TN == 2 * D_MODEL
assert (Z_COLS // PI_TN - PI_CONV_J0) * PI_CPS == CV_NC


def _proj_in_kernel(x_ref, g_ref, cs_ref, w_ref, cw_ref, cb_ref, z_ref, cv_ref, h_ref, zab_scr, a_scr):
    j = pl.program_id(1)

    @pl.when(j == 0)
    def _():
        for r in range(PI_TM // PI_RC):
            rows = slice(r * PI_RC, (r + 1) * PI_RC)
            h_ref[rows, :] = _rms(x_ref[rows, :], g_ref[...]).astype(BF16)

    def project(r, keep):
        rows = slice(r * PI_RC, (r + 1) * PI_RC)
        acc = jnp.dot(h_ref[rows, :], w_ref[...], preferred_element_type=F32)
        zb = (acc * cs_ref[...]).astype(BF16)
        z_ref[rows, :] = zb
        if keep:
            for c in range(CV_NC):
                zab_scr[0, c, rows, :] = zb[:, Z_A + c * LANES:Z_A + (c + 1) * LANES]
                zab_scr[1, c, rows, :] = zb[:, Z_B + c * LANES:Z_B + (c + 1) * LANES]

    @pl.when(j < PI_CONV_J0)
    def _():
        for r in range(PI_TM // PI_RC):
            project(r, keep=True)

    @pl.when(j >= PI_CONV_J0)
    def _():
        zero = jnp.zeros((HALO, LANES), F32)
        for k in range(PI_CPS):
            c = (j - PI_CONV_J0) * PI_CPS + k
            a_scr[k, 0:HALO, :] = zero
            a_scr[k, HALO + SEQ:, :] = zero
            for r in range(SEQ // PI_RC):
                rows = slice(r * PI_RC, (r + 1) * PI_RC)
                glu = zab_scr[0, c, rows, :].astype(F32) * _sigmoid(zab_scr[1, c, rows, :].astype(F32))
                a_scr[k, HALO + r * PI_RC:HALO + (r + 1) * PI_RC, :] = glu
        for r in range(PI_TM // PI_RC):
            project(r, keep=False)
            for k in range(PI_CPS):
                for rr in range(r * (PI_RC // CV_R), (r + 1) * (PI_RC // CV_R)):
                    acc = jnp.zeros((CV_R, LANES), F32)
                    for t in range(CONV_K):
                        start = rr * CV_R + HALO - CONV_PAD + t
                        acc = acc + cw_ref[k, t:t + 1, :] * a_scr[k, start:start + CV_R, :]
                    cv_ref[rr * CV_R:(rr + 1) * CV_R, k * LANES:(k + 1) * LANES] = acc + cb_ref[k]


def _proj_in(x2, g, colscale, w, layer, cw, cb):
    def chunk(i, j):
        return jnp.maximum(j - PI_CONV_J0, 0)

    return pl.pallas_call(
        _proj_in_kernel,
        grid=(TOKENS // PI_TM, Z_COLS // PI_TN),
        in_specs=[
            pl.BlockSpec((PI_TM, D_MODEL), lambda i, j: (i, 0), pipeline_mode=pl.Buffered(1)),
            _resident((1, D_MODEL)),
            pl.BlockSpec((1, PI_TN), lambda i, j: (0, j)),
            pl.BlockSpec((None, D_MODEL, PI_TN), lambda i, j: (layer, 0, j)),
            pl.BlockSpec((PI_CPS, CONV_K, LANES), lambda i, j: (chunk(i, j), 0, 0)),
            pl.BlockSpec((PI_CPS, 1, LANES), lambda i, j: (chunk(i, j), 0, 0)),
        ],
        out_specs=[
            pl.BlockSpec((PI_TM, PI_TN), lambda i, j: (i, j)),
            pl.BlockSpec((PI_TM, PI_CPS * LANES), lambda i, j: (i, chunk(i, j))),
        ],
        out_shape=[
            jax.ShapeDtypeStruct((TOKENS, Z_COLS), BF16),
            jax.ShapeDtypeStruct((TOKENS, D_MODEL), F32),
        ],
        scratch_shapes=[
            pltpu.VMEM((PI_TM, D_MODEL), BF16),
            pltpu.VMEM((2, CV_NC, SEQ, LANES), BF16),
            pltpu.VMEM((PI_CPS, SEQ + 2 * HALO, LANES), F32),
        ],
        compiler_params=_params(("parallel", "arbitrary")),
        name="proj_in",
    )(x2, g, colscale, w, cw, cb)


CV_TS = 1024
CV_RC = 128
CV_PC = 256


def _conv_kernel(cv_ref, gl_ref, bg_ref, lng_ref, lnb_ref, wp_ref, out_ref, act_scr):
    for r in range(CV_TS // CV_RC):
        rows = slice(r * CV_RC, (r + 1) * CV_RC)
        c = cv_ref[rows, :]
        mu = jnp.mean(c, axis=-1, keepdims=True)
        d = c - mu
        var = jnp.mean(d * d, axis=-1, keepdims=True)
        y = d * lax.rsqrt(var + EPS) * lng_ref[...] + lnb_ref[...]
        act_scr[rows, :] = (y * _sigmoid(y)).astype(BF16)

    for r in range(CV_TS // CV_PC):
        rows = slice(r * CV_PC, (r + 1) * CV_PC)
        y = jnp.dot(act_scr[rows, :], wp_ref[...], preferred_element_type=F32)
        gate = _sigmoid(gl_ref[rows, :].astype(F32) + bg_ref[...])
        out_ref[rows, :] = (gate * y).astype(BF16)


def _conv_branch(conv_raw, z2, bg, lng, lnb, wp, layer):
    return pl.pallas_call(
        _conv_kernel,
        grid=(TOKENS // CV_TS,),
        in_specs=[
            pl.BlockSpec((CV_TS, D_MODEL), lambda i: (i, 0)),
            pl.BlockSpec((CV_TS, D_MODEL), lambda i: (i, Z_GA // D_MODEL)),
            _resident((1, D_MODEL)),
            _resident((1, D_MODEL)),
            _resident((1, D_MODEL)),
            _layer_weight((D_MODEL, D_MODEL), layer),
        ],
        out_specs=pl.BlockSpec((CV_TS, D_MODEL), lambda i: (i, 0)),
        out_shape=jax.ShapeDtypeStruct((TOKENS, D_MODEL), BF16),
        scratch_shapes=[pltpu.VMEM((CV_TS, D_MODEL), BF16)],
        compiler_params=_params(("parallel",)),
        name="conv_branch",
    )(conv_raw, z2, bg, lng, lnb, wp)


AT_TQ = 256
AT_NB = 8
AT_NBUF = 2
AT_RB = 64
AT_KT = 256
AT_NKB = SEQ // AT_KT
AT_NKB_LOG2 = AT_NKB.bit_length() - 1
AT_BLK_SCR = 5
assert AT_NKB == 1 << AT_NKB_LOG2 and AT_KT == AT_TQ


def _score_softmax_pass(score_jobs, softmax_jobs, corr_ref):
    accs = [None] * len(score_jobs)
    sums = [jnp.zeros((SUBLANES, AT_TQ), F32)] * len(softmax_jobs)
    tiled = (AT_RB // SUBLANES, SUBLANES, AT_TQ)
    for t in range(AT_NKB):
        for i, (qk, s_ref) in enumerate(score_jobs):
            s = qk(t)
            for r in range(AT_KT // AT_RB):
                tile = s[r * AT_RB:(r + 1) * AT_RB, :]
                if t == 0:
                    tile = tile + corr_ref[r * AT_RB:(r + 1) * AT_RB, :]
                s_ref[t * AT_KT + r * AT_RB:t * AT_KT + (r + 1) * AT_RB, :] = tile
                part = jnp.max(tile.reshape(tiled), axis=0)
                accs[i] = part if accs[i] is None else jnp.maximum(accs[i], part)
        for i, (s_ref, m, p_ref) in enumerate(softmax_jobs):
            for r in range(AT_KT // AT_RB):
                rows = slice(t * AT_KT + r * AT_RB, t * AT_KT + (r + 1) * AT_RB)
                p = jnp.exp2(s_ref[rows, :].reshape(tiled) - m[None])
                sums[i] = sums[i] + jnp.sum(p, axis=0)
                p_ref[rows, :] = p.reshape(AT_RB, AT_TQ).astype(BF16)
    maxes = [jnp.broadcast_to(jnp.max(a, axis=0, keepdims=True), (SUBLANES, AT_TQ)) for a in accs]
    return maxes, [jnp.sum(x, axis=0, keepdims=True) for x in sums]


def _attn_kernel(lam_init, lam_ref, slope_ref, kc_ref, qc_ref, q_ref, k_ref, v_ref, g_ref, o_ref,
                 kaug_scr, qbase_scr, corr_scr, klr0_scr, klr1_scr, vt_scr, *per_block):
    klr_scr = (klr0_scr, klr1_scr)
    blk_scr = [per_block[AT_BLK_SCR * (n % AT_NBUF):AT_BLK_SCR * (n % AT_NBUF + 1)]
               for n in range(AT_NB)]
    b = pl.program_id(1)
    step = pl.program_id(2)
    lane1 = lax.broadcasted_iota(jnp.int32, (1, HEAD_W), 1) & (DH - 1)

    @pl.when((b == 0) & (step == 0))
    def _():
        is_jl = ((lane1 >= 6) & (lane1 < 9)).astype(F32)
        is_jh = ((lane1 >= 9) & (lane1 < 12)).astype(F32)
        is_il = (lane1 < 3).astype(F32)
        row = lax.broadcasted_iota(jnp.int32, (SEQ, HEAD_W), 0)
        jl = (row & (AT_TQ - 1)).astype(F32)
        jh = (row - (row & (AT_TQ - 1))).astype(F32)
        kaug_scr[...] = kc_ref[0] + is_jl * jl + is_jh * jh
        il = lax.broadcasted_iota(jnp.int32, (AT_TQ, HEAD_W), 0).astype(F32)
        qbase_scr[...] = qc_ref[0] + is_il * il
        kk = lax.broadcasted_iota(jnp.int32, (AT_TQ, AT_TQ), 0)
        qq = lax.broadcasted_iota(jnp.int32, (AT_TQ, AT_TQ), 1)
        corr_scr[...] = -2.0 * slope_ref[0] * jnp.maximum(kk - qq, 0).astype(F32)

    @pl.when(step == 0)
    def _():
        half_k = lax.broadcasted_iota(jnp.int32, (SEQ, HEAD_W), 1) >> 6
        kf = k_ref[0].astype(F32)
        kaug = kaug_scr[...]
        for mp in range(2):
            klr_scr[mp][0] = jnp.where(half_k == mp, kf, kaug).astype(BF16)
            klr_scr[mp][1] = jnp.where(half_k == mp, kf, -kaug).astype(BF16)
        for j in range(AT_NKB):
            vt_scr[j] = v_ref[0, j * AT_KT:(j + 1) * AT_KT, :].astype(F32).T.astype(BF16)

    lam_v = lam_ref[...]
    dots = jnp.sum(lam_v[0:2, :] * lam_v[2:4, :], axis=-1, keepdims=True)
    e = jnp.exp(dots)
    lam = e[0:1, :] - e[1:2, :] + lam_init

    is_q0 = ((lane1 >= 3) & (lane1 < 6)).astype(F32)
    half_q = lax.broadcasted_iota(jnp.int32, (AT_TQ, HEAD_W), 1) >> 6

    def qk_stage(n):
        vtr_scr, s_scr = blk_scr[n][0], blk_scr[n][1:3]
        qi = step * AT_NB + n
        kblk = [(qi + t) & (AT_NKB - 1) for t in range(AT_NKB)]
        for t in range(AT_NKB):
            vtr_scr[:, t * AT_KT:(t + 1) * AT_KT] = vt_scr[kblk[t]]
        qf = q_ref[0, n * AT_TQ:(n + 1) * AT_TQ, :].astype(F32)
        qaug = qbase_scr[...] + is_q0 * (qi * AT_TQ).astype(F32)
        jobs = []
        for mp in range(2):
            qat = jnp.where(half_q == mp, qf, qaug).T.astype(BF16)

            def qk(t, mp=mp, qat=qat):
                side = 0 if t == 0 else 1 - ((qi + t) >> AT_NKB_LOG2)
                src = pl.ds(pl.multiple_of(kblk[t] * AT_KT, AT_KT), AT_KT)
                return jnp.dot(klr_scr[mp][side, src, :], qat, preferred_element_type=F32)

            jobs.append((qk, s_scr[mp]))
        return jobs

    def softmax_jobs(n, mx):
        s_scr, p_scr = blk_scr[n][1:3], blk_scr[n][3:5]
        return [(s_scr[mp], mx[mp], p_scr[mp]) for mp in range(2)]

    def pv_stage(n, denom):
        vtr_scr, p_scr = blk_scr[n][0], blk_scr[n][3:5]
        outs = [jnp.dot(vtr_scr[...], p_scr[mp][...], preferred_element_type=F32) / denom[mp]
                for mp in range(2)]
        ot = outs[0] - lam * outs[1]
        ms = jnp.mean(ot * ot, axis=0, keepdims=True)
        ot = ot * lax.rsqrt(ms + EPS)
        o_ref[0, n * AT_TQ:(n + 1) * AT_TQ, :] = (ot.T * g_ref[...]).astype(BF16)

    mx, _ = _score_softmax_pass(qk_stage(0), [], corr_scr)
    for n in range(AT_NB):
        nxt = qk_stage(n + 1) if n + 1 < AT_NB else []
        mx, denom = _score_softmax_pass(nxt, softmax_jobs(n, mx), corr_scr)
        pv_stage(n, denom)


def _attention(z3, lam_vecs, slopes, kconst, qconst, gain, lam_init):
    qc, kc, vc = Z_Q // HEAD_W, Z_K // HEAD_W, Z_V // HEAD_W
    return pl.pallas_call(
        functools.partial(_attn_kernel, lam_init),
        grid=(HEADS, BATCH, SEQ // (AT_TQ * AT_NB)),
        in_specs=[
            _resident((4, DH)),
            pl.BlockSpec((1, 1, AT_TQ), lambda h, b, i: (h, 0, 0)),
            pl.BlockSpec((1, 1, HEAD_W), lambda h, b, i: (h, 0, 0)),
            pl.BlockSpec((1, 1, HEAD_W), lambda h, b, i: (h, 0, 0)),
            pl.BlockSpec((1, AT_TQ * AT_NB, HEAD_W), lambda h, b, i: (b, i, qc + h)),
            pl.BlockSpec((1, SEQ, HEAD_W), lambda h, b, i: (b, 0, kc + h)),
            pl.BlockSpec((1, SEQ, HEAD_W), lambda h, b, i: (b, 0, vc + h)),
            _resident((1, HEAD_W)),
        ],
        out_specs=pl.BlockSpec((1, AT_TQ * AT_NB, HEAD_W), lambda h, b, i: (b, i, h)),
        out_shape=jax.ShapeDtypeStruct((BATCH, SEQ, HEADS * HEAD_W), BF16),
        scratch_shapes=[
            pltpu.VMEM((SEQ, HEAD_W), F32),
            pltpu.VMEM((AT_TQ, HEAD_W), F32),
            pltpu.VMEM((AT_TQ, AT_TQ), F32),
            pltpu.VMEM((2, SEQ, HEAD_W), BF16),
            pltpu.VMEM((2, SEQ, HEAD_W), BF16),
            pltpu.VMEM((AT_NKB, HEAD_W, AT_KT), BF16),
        ] + AT_NBUF * [
            pltpu.VMEM((HEAD_W, SEQ), BF16),
            pltpu.VMEM((SEQ, AT_TQ), F32),
            pltpu.VMEM((SEQ, AT_TQ), F32),
            pltpu.VMEM((SEQ, AT_TQ), BF16),
            pltpu.VMEM((SEQ, AT_TQ), BF16),
        ],
        compiler_params=_params(("arbitrary", "arbitrary", "arbitrary")),
        name="diff_attention",
    )(lam_vecs, slopes, kconst, qconst, z3, z3, z3, gain)


SG_TS = 512
SG_PC = 256


def _sgu_kernel(su_ref, sv_ref, gl_ref, bg_ref, lng_ref, lnb_ref, ws_ref, bs_ref, wp_ref,
                out_ref, u_scr, gv_scr, sg_scr):
    for n in range(SG_TS // CHUNK):
        rows = slice(n * CHUNK, (n + 1) * CHUNK)
        v = _gelu(sv_ref[rows, :].astype(F32))
        mu = jnp.mean(v, axis=-1, keepdims=True)
        vc = v - mu
        var = jnp.mean(vc * vc, axis=-1, keepdims=True)
        gv_scr[rows, :] = (vc * lax.rsqrt(var + EPS) * lng_ref[...] + lnb_ref[...]).astype(BF16)
        u_scr[rows, :] = _gelu(su_ref[rows, :].astype(F32))
        for g in range(SGU_GROUPS):
            cols = slice(g * LANES, (g + 1) * LANES)
            mixed = jnp.dot(ws_ref[g], gv_scr[rows, cols], preferred_element_type=F32) + bs_ref[g]
            sg_scr[rows, cols] = (u_scr[rows, cols] * mixed).astype(BF16)

    for r in range(SG_TS // SG_PC):
        rows = slice(r * SG_PC, (r + 1) * SG_PC)
        y = jnp.dot(sg_scr[rows, :], wp_ref[...], preferred_element_type=F32)
        gate = _sigmoid(gl_ref[rows, :].astype(F32) + bg_ref[...])
        out_ref[rows, :] = (gate * y).astype(BF16)


def _sgu_branch(z2, bg, lng, lnb, ws, bs, wp, layer):
    def col(c):
        return pl.BlockSpec((SG_TS, D_MODEL), lambda i: (i, c))

    return pl.pallas_call(
        _sgu_kernel,
        grid=(TOKENS // SG_TS,),
        in_specs=[
            col(Z_SU // D_MODEL), col(Z_SV // D_MODEL), col(Z_GC // D_MODEL),
            _resident((1, D_MODEL)),
            _resident((1, D_MODEL)),
            _resident((1, D_MODEL)),
            _layer_weight((SGU_GROUPS, CHUNK, CHUNK), layer),
            _resident((SGU_GROUPS, CHUNK, LANES)),
            _layer_weight((D_MODEL, D_MODEL), layer),
        ],
        out_specs=pl.BlockSpec((SG_TS, D_MODEL), lambda i: (i, 0)),
        out_shape=jax.ShapeDtypeStruct((TOKENS, D_MODEL), BF16),
        scratch_shapes=[
            pltpu.VMEM((SG_TS, D_MODEL), F32),
            pltpu.VMEM((SG_TS, D_MODEL), BF16),
            pltpu.VMEM((SG_TS, D_MODEL), BF16),
        ],
        compiler_params=_params(("parallel",)),
        name="sgu_branch",
    )(z2, z2, z2, bg, lng, lnb, ws, bs, wp)


MG_TM = 1024
MG_RC = 256


def _merge_kernel(x_ref, ya_ref, o_ref, yc_ref, gl_ref, bg_ref, wa_ref, wout_ref, gpost_ref, out_ref):
    for r in range(MG_TM // MG_RC):
        rows = slice(r * MG_RC, (r + 1) * MG_RC)
        yb = jnp.dot(o_ref[rows, :], wa_ref[...], preferred_element_type=F32)
        gate = _sigmoid(gl_ref[rows, :].astype(F32) + bg_ref[...])
        mix = ya_ref[rows, :].astype(F32) + gate * yb + yc_ref[rows, :].astype(F32)
        m = jnp.dot(mix.astype(BF16), wout_ref[...], preferred_element_type=F32)
        out_ref[rows, :] = x_ref[rows, :] + _rms(m, gpost_ref[...])


def _merge(x2, ya, o, yc, z2, bg, wa, wout, gpost, layer):
    def rows(width=D_MODEL, col=0):
        return pl.BlockSpec((MG_TM, width), lambda i: (i, col))

    return pl.pallas_call(
        _merge_kernel,
        grid=(TOKENS // MG_TM,),
        in_specs=[
            rows(), rows(), rows(), rows(), rows(col=Z_GB // D_MODEL),
            _resident((1, D_MODEL)),
            _layer_weight((D_MODEL, D_MODEL), layer),
            _layer_weight((D_MODEL, D_MODEL), layer),
            _resident((1, D_MODEL)),
        ],
        out_specs=rows(),
        out_shape=jax.ShapeDtypeStruct((TOKENS, D_MODEL), F32),
        compiler_params=_params(("parallel",)),
        name="merge",
    )(x2, ya, o, yc, z2, bg, wa, wout, gpost)


FF_TM = 1024
FF_RC = 512
FF_TC = 1024


def _ffn_kernel(x_ref, gpre_ref, gpost_ref, wup_ref, wdn_ref, out_ref):
    for r in range(FF_TM // FF_RC):
        rows = slice(r * FF_RC, (r + 1) * FF_RC)
        x = x_ref[rows, :]
        h = _rms(x, gpre_ref[...]).astype(BF16)
        acc = jnp.zeros((FF_RC, D_MODEL), F32)
        for c in range(D_FF // FF_TC):
            cols = slice(c * FF_TC, (c + 1) * FF_TC)
            u = jnp.dot(h, wup_ref[:, cols], preferred_element_type=F32)
            u = jnp.square(jnp.maximum(u, 0.0)).astype(BF16)
            acc = acc + jnp.dot(u, wdn_ref[cols, :], preferred_element_type=F32)
        out_ref[rows, :] = x + _rms(acc, gpost_ref[...])


def _ffn(x2, gpre, gpost, wup, wdn, layer):
    return pl.pallas_call(
        _ffn_kernel,
        grid=(TOKENS // FF_TM,),
        in_specs=[
            pl.BlockSpec((FF_TM, D_MODEL), lambda i: (i, 0)),
            _resident((1, D_MODEL)),
            _resident((1, D_MODEL)),
            _layer_weight((D_MODEL, D_FF), layer),
            _layer_weight((D_FF, D_MODEL), layer),
        ],
        out_specs=pl.BlockSpec((FF_TM, D_MODEL), lambda i: (i, 0)),
        out_shape=jax.ShapeDtypeStruct((TOKENS, D_MODEL), F32),
        compiler_params=_params(("parallel",)),
        name="ffn",
    )(x2, gpre, gpost, wup, wdn)


def _row(v):
    return v.reshape(1, -1).astype(F32)


def kernel(x, norm_mix_pre, norm_mix_post, w_in, b_gate, conv_w, conv_b, conv_ln_g, conv_ln_b,
           lam_q1, lam_k1, lam_q2, lam_k2, subln_g, sgu_ln_g, sgu_ln_b, sgu_w, sgu_b,
           w_proj_conv, w_proj_attn, w_proj_sgu, w_out, norm_ffn_pre, norm_ffn_post,
           w_ffn_up, w_ffn_down):
    assert x.shape == (BATCH, SEQ, D_MODEL) and w_in.shape == (DEPTH, D_MODEL, Z_COLS)
    x2 = x.reshape(TOKENS, D_MODEL).astype(F32)

    colscale = jnp.ones((1, Z_COLS), F32).at[:, Z_Q:Z_Q + D_MODEL].set(DH ** -0.5 * LOG2E)
    slopes = jnp.exp2(-8.0 * jnp.arange(1, HEADS + 1, dtype=F32) / HEADS) * LOG2E
    c1 = slopes.astype(BF16).astype(F32)
    c2 = (slopes - c1).astype(BF16).astype(F32)
    c3 = (slopes - c1 - c2).astype(BF16).astype(F32)
    pieces = jnp.stack([c1, c2, c3], axis=1)
    a = jnp.arange(HEAD_W) & (DH - 1)
    kconst = jnp.where(a < 6, -pieces[:, a % 3], 0.0)[:, None, :]
    qconst = jnp.where((a >= 6) & (a < 12), pieces[:, a % 3], 0.0)[:, None, :]
    slopes = jnp.broadcast_to(slopes[:, None, None], (HEADS, 1, AT_TQ))

    w_in_b, w_conv_b, w_attn_b, w_sgu_b, w_out_b, sgu_w_b, w_up_b, w_dn_b = (
        w.astype(BF16) for w in (w_in, w_proj_conv, w_proj_attn, w_proj_sgu, w_out, sgu_w,
                                 w_ffn_up, w_ffn_down))

    for l in range(DEPTH):
        lam_init = 0.8 - 0.6 * math.exp(-0.3 * l)
        cw = conv_w[l].reshape(CONV_K, CV_NC, LANES).transpose(1, 0, 2).astype(F32)
        cb = conv_b[l].reshape(CV_NC, 1, LANES).astype(F32)
        z2, conv_raw = _proj_in(x2, _row(norm_mix_pre[l]), colscale, w_in_b, l, cw, cb)
        z3 = z2.reshape(BATCH, SEQ, Z_COLS)

        ya = _conv_branch(conv_raw, z2, _row(b_gate[l, 0:D_MODEL]),
                          _row(conv_ln_g[l]), _row(conv_ln_b[l]), w_conv_b, l)

        lam_vecs = jnp.stack([lam_q1[l], lam_q2[l], lam_k1[l], lam_k2[l]]).astype(F32)
        o = _attention(z3, lam_vecs, slopes, kconst, qconst,
                       _row(subln_g[l]) * (1.0 - lam_init), lam_init)

        bs = jnp.broadcast_to(sgu_b[l].astype(F32)[:, :, None], (SGU_GROUPS, CHUNK, LANES))
        yc = _sgu_branch(z2, _row(b_gate[l, 2 * D_MODEL:3 * D_MODEL]), _row(sgu_ln_g[l]),
                         _row(sgu_ln_b[l]), sgu_w_b, bs, w_sgu_b, l)

        x2 = _merge(x2, ya, o.reshape(TOKENS, D_MODEL), yc, z2,
                    _row(b_gate[l, D_MODEL:2 * D_MODEL]), w_attn_b, w_out_b,
                    _row(norm_mix_post[l]), l)

        x2 = _ffn(x2, _row(norm_ffn_pre[l]), _row(norm_ffn_post[l]), w_up_b, w_dn_b, l)

    return x2.reshape(BATCH, SEQ, D_MODEL).astype(x.dtype)
```
